```python
import math
import jax
import jax.numpy as jnp
from jax import lax
import numpy as np

D_MODEL = 1024
BATCH = 2
SEQ = 8192
DEPTH = 4
DEC_BATCH = 32
DEC_SEQ = 64
PAST_LEN = 4096

CHUNK = 64
N_MIXERS = 4
EPS = 1e-6
D_FF = 2816
GLA_HEADS = 4
GLA_DK = D_MODEL // 2 // GLA_HEADS
GLA_DV = D_MODEL // GLA_HEADS
GLA_RANK = 16
GLA_TAU = 16.0
BAND_HEADS = 16
BAND_DH = D_MODEL // BAND_HEADS
LEFT_CHUNKS = 8
BAND_WINDOW = LEFT_CHUNKS * CHUNK
REL_CLIP = 256
GDN_HEADS = 8
GDN_DK = D_MODEL // GDN_HEADS
GDN_DV = D_MODEL // GDN_HEADS
CONV_W = 4
GDN_CONV_CH = 2 * GDN_HEADS * GDN_DK + GDN_HEADS * GDN_DV
SB_HEADS = 16
SB_DH = D_MODEL // SB_HEADS
SB_QBLOCK = 128

kernel_name = 'hybrid_streaming_encoder_step'


def _count(m):
    return len(range(m, DEPTH, N_MIXERS))


def _chunk_len(L):
    c = min(CHUNK, L)
    assert L % c == 0, 'length must be a multiple of CHUNK or one partial chunk'
    return c


def _to_chunks(t, c):
    B, L = t.shape[:2]
    t = t.astype(jnp.float32).reshape((B, L // c, c) + t.shape[2:])
    return jnp.swapaxes(jnp.moveaxis(t, 1, 0), 2, 3)


def _from_chunks(o):
    nc, B, H, c, d = o.shape
    return jnp.moveaxis(jnp.swapaxes(o, 2, 3), 0, 1).reshape(B, nc * c, H, d)


def rms_norm(x, g):
    xf = x.astype(jnp.float32)
    y = xf * lax.rsqrt(jnp.mean(xf * xf, axis=-1, keepdims=True) + EPS)
    return (y * g.astype(jnp.float32)).astype(x.dtype)


def l2_normalize(x):
    xf = x.astype(jnp.float32)
    return xf * lax.rsqrt(jnp.sum(xf * xf, axis=-1, keepdims=True) + EPS)


def swiglu_ffn(x, w_gu, w_down):
    g, u = jnp.split(x @ w_gu, 2, axis=-1)
    return (jax.nn.silu(g) * u) @ w_down


def gla_chunked(q, k, v, log_a, s0):
    c = _chunk_len(q.shape[1])
    q, k, v, log_a = (_to_chunks(t, c) for t in (q, k, v, log_a))
    b = jnp.cumsum(log_a, axis=3)
    b_last = b[:, :, :, -1:]
    q_e = q * jnp.exp(b)
    k_e = k * jnp.exp(-b)
    k_l = k * jnp.exp(b_last - b)
    incl = jnp.tril(jnp.ones((c, c), bool))
    att = jnp.where(incl, jnp.einsum('nbhik,nbhjk->nbhij', q_e, k_e), 0.0)
    o_intra = jnp.einsum('nbhij,nbhjv->nbhiv', att, v)
    u = jnp.einsum('nbhjk,nbhjv->nbhkv', k_l, v)
    d_last = jnp.exp(b_last[:, :, :, 0])

    def step(s, inp):
        dl, uc = inp
        return dl[..., None] * s + uc, s

    s_fin, s_prev = lax.scan(step, s0.astype(jnp.float32), (d_last, u))
    o = o_intra + jnp.einsum('nbhik,nbhkv->nbhiv', q_e, s_prev)
    return _from_chunks(o), s_fin


def gla_mixer(h, s0, w_in, w_gate2, b_gate, onorm, w_out):
    B, L, _ = h.shape
    qk, vv = GLA_HEADS * GLA_DK, GLA_HEADS * GLA_DV
    q, k, v, r, g_low = jnp.split(h @ w_in, [qk, 2 * qk, 2 * qk + vv, 2 * qk + 2 * vv], axis=-1)
    log_a = jax.nn.log_sigmoid((g_low @ w_gate2 + b_gate).astype(jnp.float32)) / GLA_TAU
    hd = lambda t, d: t.reshape(B, L, GLA_HEADS, d)
    o, s_fin = gla_chunked(hd(q, GLA_DK) * GLA_DK ** -0.5, hd(k, GLA_DK), hd(v, GLA_DV), hd(log_a, GLA_DK), s0)
    o = rms_norm(o, onorm).reshape(B, L, vv).astype(h.dtype) * jax.nn.silu(r)
    return o @ w_out, s_fin


def band_project(h, w_in, q_norm, k_norm):
    B, L, _ = h.shape
    q, k, v = jnp.split(h @ w_in, 3, axis=-1)
    shp = (B, L, BAND_HEADS, BAND_DH)
    return rms_norm(q.reshape(shp), q_norm), rms_norm(k.reshape(shp), k_norm), v.reshape(shp)


def band_attend(q, k, v, q_pos, k_pos, rel_bias):
    s = jnp.einsum('bqhd,bkhd->bhqk', q.astype(jnp.float32), k.astype(jnp.float32)) * BAND_DH ** -0.5
    rel = jnp.clip(q_pos[:, None] - k_pos[None, :], -REL_CLIP, REL_CLIP) + REL_CLIP
    s = s + rel_bias[:, rel].astype(jnp.float32)[None]
    qc, kc = q_pos // CHUNK, k_pos // CHUNK
    visible = (k_pos[None, :] >= 0) & (kc[None, :] <= qc[:, None]) & (kc[None, :] >= qc[:, None] - LEFT_CHUNKS)
    p = jax.nn.softmax(jnp.where(visible[None, None], s, -jnp.inf), axis=-1)
    return jnp.einsum('bhqk,bkhd->bqhd', p, v.astype(jnp.float32))


def band_prompt(h, w_in, q_norm, k_norm, rel_bias, w_out):
    B, L, _ = h.shape
    q, k, v = band_project(h, w_in, q_norm, k_norm)
    span = BAND_WINDOW + CHUNK
    pad = ((0, 0), (BAND_WINDOW, 0), (0, 0), (0, 0))
    kp, vp = jnp.pad(k, pad), jnp.pad(v, pad)

    def one_chunk(c):
        start = c * CHUNK
        qc = lax.dynamic_slice_in_dim(q, start, CHUNK, axis=1)
        kc = lax.dynamic_slice_in_dim(kp, start, span, axis=1)
        vc = lax.dynamic_slice_in_dim(vp, start, span, axis=1)
        return band_attend(qc, kc, vc, start + jnp.arange(CHUNK), start - BAND_WINDOW + jnp.arange(span), rel_bias)

    o = lax.map(one_chunk, jnp.arange(L // CHUNK))
    o = jnp.moveaxis(o, 0, 1).reshape(B, L, D_MODEL).astype(h.dtype)
    keep = min(BAND_WINDOW, L)
    return o @ w_out, k[:, L - keep:], v[:, L - keep:]


def band_sample(h, cache_k, cache_v, past, w_in, q_norm, k_norm, rel_bias, w_out):
    B, L, _ = h.shape
    q, k, v = band_project(h, w_in, q_norm, k_norm)
    wc = cache_k.shape[1]
    kk = jnp.concatenate([cache_k.astype(k.dtype), k], axis=1)
    vv = jnp.concatenate([cache_v.astype(v.dtype), v], axis=1)
    k_pos = jnp.concatenate([past - wc + jnp.arange(wc), past + jnp.arange(L)])
    o = band_attend(q, kk, vv, past + jnp.arange(L), k_pos, rel_bias)
    o = o.reshape(B, L, D_MODEL).astype(h.dtype)
    return o @ w_out, k, v


def causal_conv(x, conv_state, w):
    L = x.shape[1]
    xp = jnp.concatenate([conv_state.astype(x.dtype), x], axis=1)
    y = sum(xp[:, i:i + L] * w[i] for i in range(CONV_W))
    return jax.nn.silu(y), xp[:, L:]


def gated_delta_chunked(q, k, v, g, beta, s0):
    c = _chunk_len(q.shape[1])
    q, k, v = (_to_chunks(t, c) for t in (q, k, v))
    g, beta = (_to_chunks(t, c) for t in (g, beta))
    q = q * (q.shape[-1] ** -0.5)
    gc = jnp.cumsum(g, axis=-1)
    incl = jnp.tril(jnp.ones((c, c), bool))
    strict = jnp.tril(jnp.ones((c, c), bool), -1)
    gam = jnp.exp(jnp.where(incl, gc[..., :, None] - gc[..., None, :], -jnp.inf))
    kb = k * beta[..., None]
    m = jnp.where(strict, jnp.einsum('nbhik,nbhjk->nbhij', kb, k) * gam, 0.0)
    eye = jnp.eye(c, dtype=jnp.float32)
    t_inv = lax.linalg.triangular_solve(eye + m, jnp.broadcast_to(eye, m.shape), left_side=True, lower=True)
    u = jnp.einsum('nbhij,nbhjv->nbhiv', t_inv, v * beta[..., None])
    w = jnp.einsum('nbhij,nbhjk->nbhik', t_inv, kb * jnp.exp(gc)[..., None])
    att = jnp.einsum('nbhik,nbhjk->nbhij', q, k) * gam
    q_e = q * jnp.exp(gc)[..., None]
    k_l = k * jnp.exp(gc[..., -1:] - gc)[..., None]
    d_last = jnp.exp(gc[..., -1])

    def step(s, inp):
        qe, wc, uc, ac, kl, dl = inp
        v_new = uc - jnp.einsum('bhik,bhkv->bhiv', wc, s)
        o = jnp.einsum('bhik,bhkv->bhiv', qe, s) + jnp.einsum('bhij,bhjv->bhiv', ac, v_new)
        s = s * dl[..., None, None] + jnp.einsum('bhjk,bhjv->bhkv', kl, v_new)
        return s, o

    s_fin, o = lax.scan(step, s0.astype(jnp.float32), (q_e, w, u, att, k_l, d_last))
    return _from_chunks(o), s_fin


def gdn_mixer(h, conv_state, s0, w_in, conv_w, a_log, dt_bias, onorm, w_out):
    B, L, _ = h.shape
    qk, vv = GDN_HEADS * GDN_DK, GDN_HEADS * GDN_DV
    qkv, gate, b_raw, a_raw = jnp.split(h @ w_in, [GDN_CONV_CH, GDN_CONV_CH + vv, GDN_CONV_CH + vv + GDN_HEADS], axis=-1)
    qkv, conv_new = causal_conv(qkv, conv_state, conv_w)
    q, k, v = jnp.split(qkv, [qk, 2 * qk], axis=-1)
    q = l2_normalize(q.reshape(B, L, GDN_HEADS, GDN_DK))
    k = l2_normalize(k.reshape(B, L, GDN_HEADS, GDN_DK))
    v = v.reshape(B, L, GDN_HEADS, GDN_DV)
    beta = jax.nn.sigmoid(b_raw.astype(jnp.float32))
    g = -jnp.exp(a_log.astype(jnp.float32)) * jax.nn.softplus(a_raw.astype(jnp.float32) + dt_bias.astype(jnp.float32))
    o, s_fin = gated_delta_chunked(q, k, v, g, beta, s0)
    o = rms_norm(o, onorm).reshape(B, L, vv).astype(h.dtype) * jax.nn.silu(gate)
    return o @ w_out, s_fin, conv_new


def sb_project(h, w_in):
    B, L, _ = h.shape
    q, k, v = jnp.split(h @ w_in, 3, axis=-1)
    shp = (B, L, SB_HEADS, SB_DH)
    return q.reshape(shp), k.reshape(shp), v.reshape(shp)


def sb_attend(q, k, v, q_pos, k_pos):
    z = jnp.einsum('bqhd,bkhd->bhqk', q.astype(jnp.float32), k.astype(jnp.float32)) * SB_DH ** -0.5
    earlier = k_pos[None, :] < q_pos[:, None]
    log_1mb = jnp.where(earlier, jax.nn.log_sigmoid(-z), 0.0)
    surv = lax.cumsum(log_1mb, axis=3, reverse=True) - log_1mb
    a = jnp.where(earlier, jnp.exp(jax.nn.log_sigmoid(z) + surv), 0.0)
    return jnp.einsum('bhqk,bkhd->bqhd', a, v.astype(jnp.float32))


def sb_prompt(h, w_in, w_out):
    B, L, _ = h.shape
    q, k, v = sb_project(h, w_in)
    pos = jnp.arange(L)

    def block(b):
        start = b * SB_QBLOCK
        qb = lax.dynamic_slice_in_dim(q, start, SB_QBLOCK, axis=1)
        return sb_attend(qb, k, v, start + jnp.arange(SB_QBLOCK), pos)

    o = lax.map(block, jnp.arange(L // SB_QBLOCK))
    o = jnp.moveaxis(o, 0, 1).reshape(B, L, D_MODEL).astype(h.dtype)
    return o @ w_out, k, v


def sb_sample(h, cache_k, cache_v, w_in, w_out):
    B, L, _ = h.shape
    past = cache_k.shape[1]
    q, k, v = sb_project(h, w_in)
    kk = jnp.concatenate([cache_k.astype(k.dtype), k], axis=1)
    vv = jnp.concatenate([cache_v.astype(v.dtype), v], axis=1)
    o = sb_attend(q, kk, vv, past + jnp.arange(L), jnp.arange(past + L))
    o = o.reshape(B, L, D_MODEL).astype(h.dtype)
    return o @ w_out, k, v


def setup_inputs(seed: int = 0) -> dict:
    key = jax.random.key(seed)
    keys = iter(jax.random.split(key, 48))
    f32 = jnp.float32

    def nrm(shape, scale):
        return jax.random.normal(next(keys), shape, f32) * scale

    def gain(shape):
        return 1.0 + nrm(shape, 0.02)

    n_gla, n_band, n_gdn, n_sb = (_count(m) for m in range(N_MIXERS))
    band_rows = min(BAND_WINDOW, PAST_LEN)
    d = D_MODEL
    gla_cols = 2 * GLA_HEADS * GLA_DK + 2 * GLA_HEADS * GLA_DV + GLA_RANK
    gdn_cols = GDN_CONV_CH + GDN_HEADS * GDN_DV + 2 * GDN_HEADS
    dt = jnp.exp(jax.random.uniform(next(keys), (n_gdn, GDN_HEADS), f32, math.log(1e-3), math.log(1e-1)))
    a_log = jnp.log(jax.random.uniform(next(keys), (n_gdn, GDN_HEADS), f32, 1.0, 16.0))
    return {
        'x_prompt': nrm((BATCH, SEQ, d), 1.0),
        'x_sample': nrm((DEC_BATCH, DEC_SEQ, d), 1.0),
        'state_gla': nrm((n_gla, DEC_BATCH, GLA_HEADS, GLA_DK, GLA_DV), 0.1),
        'cache_band_k': nrm((n_band, DEC_BATCH, band_rows, BAND_HEADS, BAND_DH), 1.0),
        'cache_band_v': nrm((n_band, DEC_BATCH, band_rows, BAND_HEADS, BAND_DH), 1.0),
        'state_gdn': nrm((n_gdn, DEC_BATCH, GDN_HEADS, GDN_DK, GDN_DV), 0.1),
        'state_gdn_conv': nrm((n_gdn, DEC_BATCH, CONV_W - 1, GDN_CONV_CH), 1.0),
        'cache_sb_k': nrm((n_sb, DEC_BATCH, PAST_LEN, SB_HEADS, SB_DH), 1.0),
        'cache_sb_v': nrm((n_sb, DEC_BATCH, PAST_LEN, SB_HEADS, SB_DH), 1.0),
        'ffn1_norm': gain((DEPTH, d)),
        'ffn1_w_gu': nrm((DEPTH, d, 2 * D_FF), d ** -0.5),
        'ffn1_w_down': nrm((DEPTH, D_FF, d), D_FF ** -0.5),
        'mix_norm': gain((DEPTH, d)),
        'ffn2_norm': gain((DEPTH, d)),
        'ffn2_w_gu': nrm((DEPTH, d, 2 * D_FF), d ** -0.5),
        'ffn2_w_down': nrm((DEPTH, D_FF, d), D_FF ** -0.5),
        'gla_w_in': nrm((n_gla, d, gla_cols), d ** -0.5),
        'gla_w_gate2': nrm((n_gla, GLA_RANK, GLA_HEADS * GLA_DK), GLA_RANK ** -0.5),
        'gla_b_gate': nrm((n_gla, GLA_HEADS * GLA_DK), 0.1),
        'gla_onorm': gain((n_gla, GLA_DV)),
        'gla_w_out': nrm((n_gla, GLA_HEADS * GLA_DV, d), (GLA_HEADS * GLA_DV) ** -0.5),
        'band_w_in': nrm((n_band, d, 3 * d), d ** -0.5),
        'band_q_norm': gain((n_band, BAND_DH)),
        'band_k_norm': gain((n_band, BAND_DH)),
        'band_rel_bias': nrm((n_band, BAND_HEADS, 2 * REL_CLIP + 1), 0.1),
        'band_w_out': nrm((n_band, d, d), d ** -0.5),
        'gdn_w_in': nrm((n_gdn, d, gdn_cols), d ** -0.5),
        'gdn_conv_w': nrm((n_gdn, CONV_W, GDN_CONV_CH), CONV_W ** -0.5),
        'gdn_a_log': a_log,
        'gdn_dt_bias': dt + jnp.log(-jnp.expm1(-dt)),
        'gdn_onorm': gain((n_gdn, GDN_DV)),
        'gdn_w_out': nrm((n_gdn, GDN_HEADS * GDN_DV, d), (GDN_HEADS * GDN_DV) ** -0.5),
        'sb_w_in': nrm((n_sb, d, 3 * d), d ** -0.5),
        'sb_w_out': nrm((n_sb, d, d), d ** -0.5),
    }


def reference(x_prompt, x_sample, state_gla, cache_band_k, cache_band_v, state_gdn, state_gdn_conv,
              cache_sb_k, cache_sb_v, ffn1_norm, ffn1_w_gu, ffn1_w_down, mix_norm, ffn2_norm,
              ffn2_w_gu, ffn2_w_down, gla_w_in, gla_w_gate2, gla_b_gate, gla_onorm, gla_w_out,
              band_w_in, band_q_norm, band_k_norm, band_rel_bias, band_w_out, gdn_w_in, gdn_conv_w,
              gdn_a_log, gdn_dt_bias, gdn_onorm, gdn_w_out, sb_w_in, sb_w_out):
    past = cache_sb_k.shape[2]
    xp, xs = x_prompt, x_sample
    bp = xp.shape[0]
    gla_p, gla_s = [], []
    band_kp, band_vp, band_ks, band_vs = [], [], [], []
    gdn_p, gdn_cp, gdn_s, gdn_cs = [], [], [], []
    sb_kp, sb_vp, sb_ks, sb_vs = [], [], [], []
    for i in range(DEPTH):
        m, j = i % N_MIXERS, i // N_MIXERS
        xp = xp + 0.5 * swiglu_ffn(rms_norm(xp, ffn1_norm[i]), ffn1_w_gu[i], ffn1_w_down[i])
        xs = xs + 0.5 * swiglu_ffn(rms_norm(xs, ffn1_norm[i]), ffn1_w_gu[i], ffn1_w_down[i])
        hp, hs = rms_norm(xp, mix_norm[i]), rms_norm(xs, mix_norm[i])
        if m == 0:
            w = (gla_w_in[j], gla_w_gate2[j], gla_b_gate[j], gla_onorm[j], gla_w_out[j])
            zero = jnp.zeros((bp, GLA_HEADS, GLA_DK, GLA_DV), jnp.float32)
            yp, sp = gla_mixer(hp, zero, *w)
            ys, ss = gla_mixer(hs, state_gla[j], *w)
            gla_p.append(sp)
            gla_s.append(ss)
        elif m == 1:
            w = (band_w_in[j], band_q_norm[j], band_k_norm[j], band_rel_bias[j], band_w_out[j])
            yp, kp, vp = band_prompt(hp, *w)
            ys, ks, vs = band_sample(hs, cache_band_k[j], cache_band_v[j], past, *w)
            band_kp.append(kp)
            band_vp.append(vp)
            band_ks.append(ks)
            band_vs.append(vs)
        elif m == 2:
            w = (gdn_w_in[j], gdn_conv_w[j], gdn_a_log[j], gdn_dt_bias[j], gdn_onorm[j], gdn_w_out[j])
            zs = jnp.zeros((bp, GDN_HEADS, GDN_DK, GDN_DV), jnp.float32)
            zc = jnp.zeros((bp, CONV_W - 1, GDN_CONV_CH), hp.dtype)
            yp, sp, cp = gdn_mixer(hp, zc, zs, *w)
            ys, ss, cs = gdn_mixer(hs, state_gdn_conv[j], state_gdn[j], *w)
            gdn_p.append(sp)
            gdn_cp.append(cp)
            gdn_s.append(ss)
            gdn_cs.append(cs)
        else:
            yp, kp, vp = sb_prompt(hp, sb_w_in[j], sb_w_out[j])
            ys, ks, vs = sb_sample(hs, cache_sb_k[j], cache_sb_v[j], sb_w_in[j], sb_w_out[j])
            sb_kp.append(kp)
            sb_vp.append(vp)
            sb_ks.append(ks)
            sb_vs.append(vs)
        xp = xp + yp
        xs = xs + ys
        xp = xp + 0.5 * swiglu_ffn(rms_norm(xp, ffn2_norm[i]), ffn2_w_gu[i], ffn2_w_down[i])
        xs = xs + 0.5 * swiglu_ffn(rms_norm(xs, ffn2_norm[i]), ffn2_w_gu[i], ffn2_w_down[i])
    return (xp, xs,
            jnp.stack(gla_p), jnp.stack(gla_s),
            jnp.stack(band_kp), jnp.stack(band_vp), jnp.stack(band_ks), jnp.stack(band_vs),
            jnp.stack(gdn_p), jnp.stack(gdn_cp), jnp.stack(gdn_s), jnp.stack(gdn_cs),
            jnp.stack(sb_kp), jnp.stack(sb_vp), jnp.stack(sb_ks), jnp.stack(sb_vs))
```

```python
import functools

import jax
import jax.numpy as jnp
from jax import lax
from jax.experimental import pallas as pl
from jax.experimental.pallas import tpu as pltpu

F32 = jnp.float32
BF16 = jnp.bfloat16

D_MODEL = 1024
CHUNK = 64
EPS = 1e-6
D_FF = 2816
GLA_HEADS, GLA_DK, GLA_DV, GLA_RANK, GLA_TAU = 4, 128, 256, 16, 16.0
BAND_HEADS, BAND_DH, LEFT_CHUNKS, REL_CLIP = 16, 64, 8, 256
BAND_WINDOW = LEFT_CHUNKS * CHUNK
BAND_SPAN = BAND_WINDOW + CHUNK
GDN_HEADS, GDN_DK, GDN_DV, CONV_W = 8, 128, 128, 4
GDN_CONV_CH = 2 * GDN_HEADS * GDN_DK + GDN_HEADS * GDN_DV
SB_HEADS, SB_DH = 16, 64

LANES = 128
SUBLANES = 8
VMEM_LIMIT = 56 * 1024 * 1024
ROW_TILE = 512
FF_TILE = 1408
SB_TQ = 256
SB_TK = 256


def _params(*sem):
    return pltpu.CompilerParams(dimension_semantics=sem, vmem_limit_bytes=VMEM_LIMIT)


def _dot(a, b):
    return jnp.dot(a, b, preferred_element_type=F32)


def _dot_nt(a, b):
    return lax.dot_general(a, b, (((1,), (1,)), ((), ())), preferred_element_type=F32)


def _dot_tn(a, b):
    return lax.dot_general(a, b, (((0,), (0,)), ((), ())), preferred_element_type=F32)


def _split3(x):
    h1 = x.astype(BF16)
    r1 = x - h1.astype(F32)
    h2 = r1.astype(BF16)
    h3 = (r1 - h2.astype(F32)).astype(BF16)
    return h1, h2, h3


def _tri_dot(tri, x):
    h1, h2, h3 = _split3(x)
    return _dot(tri, h1) + _dot(tri, h2) + _dot(tri, h3)


def _dot_f32(a, b):
    a1 = a.astype(BF16)
    a2 = (a - a1.astype(F32)).astype(BF16)
    b1 = b.astype(BF16)
    b2 = (b - b1.astype(F32)).astype(BF16)
    return _dot(a1, b1) + (_dot(a1, b2) + _dot(a2, b1)) + _dot(a2, b2)


def _sigmoid(x):
    return 1.0 / (1.0 + jnp.exp(-x))


def _silu(x):
    return x * _sigmoid(x)


def _softplus(x):
    return jnp.maximum(x, 0.0) + jnp.log1p(jnp.exp(-jnp.abs(x)))


def _log_sigmoid(x):
    return -_softplus(-x)


def _rms(x, g):
    return x * lax.rsqrt(jnp.mean(x * x, axis=-1, keepdims=True) + EPS) * g


def _iota2(shape, dim):
    return lax.broadcasted_iota(jnp.int32, shape, dim)


def _ffn_kernel(x_ref, g_ref, wg_ref, wu_ref, wd_ref, o_ref, xn_ref, acc_ref):
    j = pl.program_id(1)

    @pl.when(j == 0)
    def _():
        xn_ref[...] = _rms(x_ref[...], g_ref[...]).astype(BF16)
        acc_ref[...] = jnp.zeros_like(acc_ref)

    xn = xn_ref[...]
    g = _dot(xn, wg_ref[...])
    u = _dot(xn, wu_ref[...])
    h = (_silu(g) * u).astype(BF16)
    acc_ref[...] += _dot(h, wd_ref[...])

    @pl.when(j == pl.num_programs(1) - 1)
    def _():
        o_ref[...] = x_ref[...] + 0.5 * acc_ref[...]


def _ffn(x, gain, w_gu, w_down):
    t, d = x.shape
    nj = D_FF // FF_TILE
    return pl.pallas_call(
        _ffn_kernel,
        out_shape=jax.ShapeDtypeStruct((t, d), F32),
        grid=(t // ROW_TILE, nj),
        in_specs=[
            pl.BlockSpec((ROW_TILE, d), lambda i, j: (i, 0)),
            pl.BlockSpec((1, d), lambda i, j: (0, 0)),
            pl.BlockSpec((d, FF_TILE), lambda i, j: (0, j)),
            pl.BlockSpec((d, FF_TILE), lambda i, j: (0, j + nj)),
            pl.BlockSpec((FF_TILE, d), lambda i, j: (j, 0)),
        ],
        out_specs=pl.BlockSpec((ROW_TILE, d), lambda i, j: (i, 0)),
        scratch_shapes=[pltpu.VMEM((ROW_TILE, d), BF16), pltpu.VMEM((ROW_TILE, d), F32)],
        compiler_params=_params("parallel", "arbitrary"),
        name="ffn",
    )(x, gain.reshape(1, d), w_gu, w_gu, w_down)


def _normproj_kernel(x_ref, g_ref, w_ref, o_ref, ob_ref, xn_ref):
    @pl.when(pl.program_id(1) == 0)
    def _():
        xn_ref[...] = _rms(x_ref[...], g_ref[...]).astype(BF16)

    y = _dot(xn_ref[...], w_ref[...])
    o_ref[...] = y
    ob_ref[...] = y.astype(BF16)


def _normproj(x, gain, w, col_tile):
    t, d = x.shape
    n = w.shape[1]
    return pl.pallas_call(
        _normproj_kernel,
        out_shape=(jax.ShapeDtypeStruct((t, n), F32), jax.ShapeDtypeStruct((t, n), BF16)),
        grid=(t // ROW_TILE, n // col_tile),
        in_specs=[
            pl.BlockSpec((ROW_TILE, d), lambda i, j: (i, 0)),
            pl.BlockSpec((1, d), lambda i, j: (0, 0)),
            pl.BlockSpec((d, col_tile), lambda i, j: (0, j)),
        ],
        out_specs=(pl.BlockSpec((ROW_TILE, col_tile), lambda i, j: (i, j)),
                   pl.BlockSpec((ROW_TILE, col_tile), lambda i, j: (i, j))),
        scratch_shapes=[pltpu.VMEM((ROW_TILE, d), BF16)],
        compiler_params=_params("parallel", "arbitrary"),
        name="normproj",
    )(x, gain.reshape(1, d), w)


def _outproj_kernel(res_ref, a_ref, w_ref, o_ref):
    o_ref[...] = res_ref[...] + _dot(a_ref[...], w_ref[...])


def _outproj(res, a, w):
    t, d = res.shape
    return pl.pallas_call(
        _outproj_kernel,
        out_shape=jax.ShapeDtypeStruct((t, d), F32),
        grid=(t // ROW_TILE,),
        in_specs=[
            pl.BlockSpec((ROW_TILE, d), lambda i: (i, 0)),
            pl.BlockSpec((ROW_TILE, a.shape[1]), lambda i: (i, 0)),
            pl.BlockSpec(w.shape, lambda i: (0, 0)),
        ],
        out_specs=pl.BlockSpec((ROW_TILE, d), lambda i: (i, 0)),
        compiler_params=_params("parallel"),
        name="outproj",
    )(res, a, w)


def _gla_kernel(*refs, has_init):
    if has_init:
        (q_ref, k_ref, v_ref, r_ref, gl_ref, wg2_ref, bg_ref, on_ref, s0_ref,
         o_ref, sfin_ref, st_ref) = refs
    else:
        (q_ref, k_ref, v_ref, r_ref, gl_ref, wg2_ref, bg_ref, on_ref,
         o_ref, sfin_ref, st_ref) = refs
    c = pl.program_id(1)

    @pl.when(c == 0)
    def _():
        for h in range(GLA_HEADS):
            if has_init:
                st_ref[h] = s0_ref[0, h].T
            else:
                st_ref[h] = jnp.zeros((GLA_DV, GLA_DK), F32)

    incl = _iota2((CHUNK, CHUNK), 0) >= _iota2((CHUNK, CHUNK), 1)
    tri = jnp.where(incl, 1.0, 0.0).astype(BF16)
    x = _dot(gl_ref[...].astype(BF16), wg2_ref[...]) + bg_ref[...]
    log_a = _log_sigmoid(x) * (1.0 / GLA_TAU)
    b = _tri_dot(tri, log_a)
    b_last = b[CHUNK - 1:CHUNK, :]
    q = q_ref[...] * (GLA_DK ** -0.5)
    k = k_ref[...]
    q_e = (q * jnp.exp(b)).astype(BF16)
    k_e = (k * jnp.exp(-b)).astype(BF16)
    k_l = (k * jnp.exp(b_last - b)).astype(BF16)
    d_last = jnp.exp(b_last)
    gain = on_ref[...]
    for h in range(GLA_HEADS):
        ks = slice(h * GLA_DK, (h + 1) * GLA_DK)
        vs = slice(h * GLA_DV, (h + 1) * GLA_DV)
        v = v_ref[:, vs].astype(BF16)
        st = st_ref[h]
        att = jnp.where(incl, _dot_nt(q_e[:, ks], k_e[:, ks]), 0.0)
        o = _dot(att.astype(BF16), v) + _dot_nt(q_e[:, ks], st.astype(BF16))
        st_ref[h] = st * d_last[:, ks] + _dot_tn(v, k_l[:, ks])
        o = _rms(o, gain) * _silu(r_ref[:, vs])
        o_ref[:, vs] = o.astype(BF16)

    @pl.when(c == pl.num_programs(1) - 1)
    def _():
        for h in range(GLA_HEADS):
            sfin_ref[0, h] = st_ref[h].T


def _gla(proj, w_gate2, b_gate, onorm, s0, nb, nc, row_off):
    qk = GLA_HEADS * GLA_DK
    vv = GLA_HEADS * GLA_DV
    row = lambda b, c: row_off + b * nc + c
    in_specs = [
        pl.BlockSpec((CHUNK, qk), lambda b, c: (row(b, c), 0)),
        pl.BlockSpec((CHUNK, qk), lambda b, c: (row(b, c), 1)),
        pl.BlockSpec((CHUNK, vv), lambda b, c: (row(b, c), 1)),
        pl.BlockSpec((CHUNK, vv), lambda b, c: (row(b, c), 2)),
        pl.BlockSpec((CHUNK, LANES), lambda b, c: (row(b, c), (2 * qk + 2 * vv) // LANES)),
        pl.BlockSpec((LANES, qk), lambda b, c: (0, 0)),
        pl.BlockSpec((1, qk), lambda b, c: (0, 0)),
        pl.BlockSpec((1, GLA_DV), lambda b, c: (0, 0)),
    ]
    args = [proj, proj, proj, proj, proj, w_gate2, b_gate, onorm]
    if s0 is not None:
        in_specs.append(pl.BlockSpec((1, GLA_HEADS, GLA_DK, GLA_DV), lambda b, c: (b, 0, 0, 0)))
        args.append(s0)
    return pl.pallas_call(
        functools.partial(_gla_kernel, has_init=s0 is not None),
        out_shape=(jax.ShapeDtypeStruct((nb * nc * CHUNK, vv), BF16),
                   jax.ShapeDtypeStruct((nb, GLA_HEADS, GLA_DK, GLA_DV), F32)),
        grid=(nb, nc),
        in_specs=in_specs,
        out_specs=(pl.BlockSpec((CHUNK, vv), lambda b, c: (b * nc + c, 0)),
                   pl.BlockSpec((1, GLA_HEADS, GLA_DK, GLA_DV), lambda b, c: (b, 0, 0, 0))),
        scratch_shapes=[pltpu.VMEM((GLA_HEADS, GLA_DV, GLA_DK), F32)],
        compiler_params=_params("parallel", "arbitrary"),
        name="gla",
    )(*args)


def _pair_head_norm(x, gain2):
    low = _iota2(x.shape, 1) < BAND_DH
    sq = x * x
    s_lo = jnp.sum(jnp.where(low, sq, 0.0), axis=-1, keepdims=True)
    s_hi = jnp.sum(jnp.where(low, 0.0, sq), axis=-1, keepdims=True)
    ms = jnp.where(low, s_lo, s_hi) * (1.0 / BAND_DH)
    return x * lax.rsqrt(ms + EPS) * gain2


def _band_prep_kernel(q_ref, k_ref, v_ref, qg_ref, kg_ref, qn_ref, kn_ref, knb_ref, vb_ref):
    for p in range(D_MODEL // LANES):
        ls = slice(p * LANES, (p + 1) * LANES)
        qn_ref[:, ls] = _pair_head_norm(q_ref[:, ls], qg_ref[...]).astype(BF16)
        kn = _pair_head_norm(k_ref[:, ls], kg_ref[...])
        kn_ref[:, ls] = kn
        knb_ref[:, ls] = kn.astype(BF16)
    vb_ref[...] = v_ref[...].astype(BF16)


def _band_prep(proj, q_gain, k_gain):
    t = proj.shape[0]
    d = D_MODEL
    col = lambda j: pl.BlockSpec((ROW_TILE, d), lambda i: (i, j))
    vec = pl.BlockSpec((1, LANES), lambda i: (0, 0))
    out = pl.BlockSpec((ROW_TILE, d), lambda i: (i, 0))
    return pl.pallas_call(
        _band_prep_kernel,
        out_shape=(jax.ShapeDtypeStruct((t, d), BF16), jax.ShapeDtypeStruct((t, d), F32),
                   jax.ShapeDtypeStruct((t, d), BF16), jax.ShapeDtypeStruct((t, d), BF16)),
        grid=(t // ROW_TILE,),
        in_specs=[col(0), col(1), col(2), vec, vec],
        out_specs=(out, out, out, out),
        compiler_params=_params("parallel"),
        name="band_prep",
    )(proj, proj, proj, jnp.tile(q_gain, 2).reshape(1, LANES), jnp.tile(k_gain, 2).reshape(1, LANES))


def _band_attn_kernel(q_ref, kp_ref, kc_ref, vp_ref, vc_ref, bias_ref, o_ref, kall_ref, vall_ref,
                      *, nq, prompt):
    if prompt:
        kall_ref[0:BAND_WINDOW] = kp_ref[...]
        vall_ref[0:BAND_WINDOW] = vp_ref[...]
    else:
        kall_ref[0:BAND_WINDOW] = kp_ref[0].astype(BF16)
        vall_ref[0:BAND_WINDOW] = vp_ref[0].astype(BF16)
    kall_ref[BAND_WINDOW:] = kc_ref[...]
    vall_ref[BAND_WINDOW:] = vc_ref[...]
    first_pos = (pl.program_id(1) - 1) * BAND_WINDOW if prompt else 0
    low = _iota2((CHUNK, LANES), 1) < BAND_DH
    key_idx = _iota2((CHUNK, BAND_SPAN), 1)

    def chunk(t, carry):
        r0 = pl.multiple_of(t * CHUNK, CHUNK)
        for p in range(D_MODEL // LANES):
            ls = slice(p * LANES, (p + 1) * LANES)
            q2 = q_ref[pl.ds(r0, CHUNK), ls]
            k2 = kall_ref[pl.ds(r0, BAND_SPAN), ls]
            v2 = vall_ref[pl.ds(r0, BAND_SPAN), ls]
            pv = []
            for half in range(2):
                qm = jnp.where(low, q2, 0) if half == 0 else jnp.where(low, 0, q2)
                s = _dot_nt(qm, k2) * (BAND_DH ** -0.5) + bias_ref[2 * p + half]
                if prompt:
                    s = jnp.where(first_pos + r0 + key_idx >= 0, s, -jnp.inf)
                e = jnp.exp(s - jnp.max(s, axis=-1, keepdims=True))
                prob = e / jnp.sum(e, axis=-1, keepdims=True)
                pv.append(_dot(prob.astype(BF16), v2))
            o_ref[pl.ds(r0, CHUNK), ls] = jnp.where(low, pv[0], pv[1]).astype(BF16)
        return carry

    lax.fori_loop(0, nq, chunk, 0)


def _band_attn_prompt(qn, knb, vb, bias, nb, seq):
    d = D_MODEL
    blk = BAND_WINDOW
    nblk = seq // blk
    cur = pl.BlockSpec((blk, d), lambda b, i: (b * nblk + i, 0))
    prev = pl.BlockSpec((blk, d), lambda b, i: (b * nblk + jnp.maximum(i - 1, 0), 0))
    return pl.pallas_call(
        functools.partial(_band_attn_kernel, nq=blk // CHUNK, prompt=True),
        out_shape=jax.ShapeDtypeStruct((nb * seq, d), BF16),
        grid=(nb, nblk),
        in_specs=[cur, prev, cur, prev, cur,
                  pl.BlockSpec(bias.shape, lambda b, i: (0, 0, 0))],
        out_specs=cur,
        scratch_shapes=[pltpu.VMEM((2 * blk, d), BF16), pltpu.VMEM((2 * blk, d), BF16)],
        compiler_params=_params("parallel", "arbitrary"),
        name="band_attn_prompt",
    )(qn, knb, knb, vb, vb, bias)


def _band_attn_sample(qn, knb, vb, cache_k, cache_v, bias, nb, row_off):
    d = D_MODEL
    new = pl.BlockSpec((CHUNK, d), lambda b: (row_off + b, 0))
    old = pl.BlockSpec((1, BAND_WINDOW, d), lambda b: (b, 0, 0))
    return pl.pallas_call(
        functools.partial(_band_attn_kernel, nq=1, prompt=False),
        out_shape=jax.ShapeDtypeStruct((nb * CHUNK, d), BF16),
        grid=(nb,),
        in_specs=[new, old, new, old, new, pl.BlockSpec(bias.shape, lambda b: (0, 0, 0))],
        out_specs=pl.BlockSpec((CHUNK, d), lambda b: (b, 0)),
        scratch_shapes=[pltpu.VMEM((BAND_SPAN, d), BF16), pltpu.VMEM((BAND_SPAN, d), BF16)],
        compiler_params=_params("parallel"),
        name="band_attn_sample",
    )(qn, cache_k, knb, cache_v, vb, bias)


def _gdn_prep_kernel(*refs, has_init):
    if has_init:
        x_ref, raw_ref, cw_ref, par_ref, c0_ref, q_ref, k_ref, v_ref, bg_ref, carry_ref = refs
    else:
        x_ref, raw_ref, cw_ref, par_ref, q_ref, k_ref, v_ref, bg_ref, carry_ref = refs
    tb = x_ref.shape[0]

    @pl.when(pl.program_id(1) == 0)
    def _():
        if has_init:
            carry_ref[...] = c0_ref[0]
        else:
            carry_ref[...] = jnp.zeros_like(carry_ref)

    x = x_ref[...]
    w = cw_ref[...]
    x8 = x[0:SUBLANES]
    c8 = carry_ref[...]
    row8 = _iota2(x8.shape, 0)
    y = x * w[CONV_W - 1:CONV_W]
    y8 = x8 * w[CONV_W - 1:CONV_W]
    for s in range(1, CONV_W):
        ws = w[CONV_W - 1 - s:CONV_W - s]
        y = y + pltpu.roll(x, s, 0) * ws
        y8 = y8 + jnp.where(row8 < s, pltpu.roll(c8, s, 0), pltpu.roll(x8, s, 0)) * ws
    carry_ref[...] = x[tb - SUBLANES:tb]
    if tb > SUBLANES:
        y = jnp.concatenate([y8, y[SUBLANES:]], axis=0)
    else:
        y = y8
    y = _silu(y)
    qk = GDN_HEADS * GDN_DK
    for h in range(GDN_HEADS):
        qh = y[:, h * GDN_DK:(h + 1) * GDN_DK]
        kh = y[:, qk + h * GDN_DK:qk + (h + 1) * GDN_DK]
        q_ref[h] = qh * lax.rsqrt(jnp.sum(qh * qh, axis=-1, keepdims=True) + EPS)
        k_ref[h] = kh * lax.rsqrt(jnp.sum(kh * kh, axis=-1, keepdims=True) + EPS)
        v_ref[h] = y[:, 2 * qk + h * GDN_DV:2 * qk + (h + 1) * GDN_DV]
    raw = raw_ref[...]
    beta = _sigmoid(raw)
    g = par_ref[0:1, :] * _softplus(raw + par_ref[1:2, :])
    bg_ref[...] = jnp.where(_iota2(raw.shape, 1) < GDN_HEADS, beta, g)


def _gdn_prep(proj, conv_w, par, c0, nb, seq, row_off, tb):
    nblk = seq // tb
    base = row_off // tb
    row = lambda b, i: base + b * nblk + i
    in_specs = [
        pl.BlockSpec((tb, GDN_CONV_CH), lambda b, i: (row(b, i), 0)),
        pl.BlockSpec((tb, LANES), lambda b, i: (row(b, i), (GDN_CONV_CH + GDN_HEADS * GDN_DV) // LANES)),
        pl.BlockSpec((CONV_W, GDN_CONV_CH), lambda b, i: (0, 0)),
        pl.BlockSpec((2, LANES), lambda b, i: (0, 0)),
    ]
    args = [proj, proj, conv_w, par]
    if c0 is not None:
        in_specs.append(pl.BlockSpec((1, SUBLANES, GDN_CONV_CH), lambda b, i: (b, 0, 0)))
        args.append(c0)
    n = nb * seq
    head = pl.BlockSpec((GDN_HEADS, tb, GDN_DK), lambda b, i: (0, b * nblk + i, 0))
    return pl.pallas_call(
        functools.partial(_gdn_prep_kernel, has_init=c0 is not None),
        out_shape=(jax.ShapeDtypeStruct((GDN_HEADS, n, GDN_DK), F32),
                   jax.ShapeDtypeStruct((GDN_HEADS, n, GDN_DK), F32),
                   jax.ShapeDtypeStruct((GDN_HEADS, n, GDN_DV), F32),
                   jax.ShapeDtypeStruct((n, LANES), F32)),
        grid=(nb, nblk),
        in_specs=in_specs,
        out_specs=(head, head, head, pl.BlockSpec((tb, LANES), lambda b, i: (b * nblk + i, 0))),
        scratch_shapes=[pltpu.VMEM((SUBLANES, GDN_CONV_CH), F32)],
        compiler_params=_params("parallel", "arbitrary"),
        name="gdn_prep",
    )(*args)


def _gdn_kernel(*refs, has_init):
    if has_init:
        q_ref, k_ref, v_ref, bg_ref, gate_ref, on_ref, s0_ref, o_ref, sfin_ref, s_ref = refs
    else:
        q_ref, k_ref, v_ref, bg_ref, gate_ref, on_ref, o_ref, sfin_ref, s_ref = refs
    c = pl.program_id(1)

    @pl.when(c == 0)
    def _():
        if has_init:
            s_ref[...] = s0_ref[0]
        else:
            s_ref[...] = jnp.zeros_like(s_ref)

    ri = _iota2((CHUNK, CHUNK), 0)
    ci = _iota2((CHUNK, CHUNK), 1)
    incl = ri >= ci
    strict = ri > ci
    tri = jnp.where(incl, 1.0, 0.0).astype(BF16)
    eye = jnp.where(ri == ci, 1.0, 0.0)
    bg = bg_ref[...]
    cs = _tri_dot(tri, bg)
    cs_t = cs.T
    gain = on_ref[...]
    for h in range(GDN_HEADS):
        beta = bg[:, h:h + 1]
        gcol = cs[:, GDN_HEADS + h:GDN_HEADS + h + 1]
        grow = cs_t[GDN_HEADS + h:GDN_HEADS + h + 1, :]
        glast = gcol[CHUNK - 1:CHUNK, :]
        gam = jnp.exp(jnp.where(incl, gcol - grow, -jnp.inf))
        egc = jnp.exp(gcol)
        k = k_ref[h]
        kb16 = k.astype(BF16)
        kb = k * beta
        m = jnp.where(strict, _dot_nt(kb.astype(BF16), kb16) * gam, 0.0)
        t_inv = eye - m
        pw = m
        for _ in range(5):
            pw = _dot_f32(pw, pw)
            t_inv = t_inv + _dot_f32(t_inv, pw)
        t16 = t_inv.astype(BF16)
        u = _dot(t16, (v_ref[h] * beta).astype(BF16))
        w = _dot(t16, (kb * egc).astype(BF16))
        q = q_ref[h] * (GDN_DK ** -0.5)
        att = _dot_nt(q.astype(BF16), kb16) * gam
        q_e = (q * egc).astype(BF16)
        k_l = (k * jnp.exp(glast - gcol)).astype(BF16)
        s = s_ref[h]
        s16 = s.astype(BF16)
        v_new = u - _dot(w.astype(BF16), s16)
        o = _dot(q_e, s16) + _dot(att.astype(BF16), v_new.astype(BF16))
        s_ref[h] = s * jnp.exp(glast) + _dot_tn(k_l, v_new.astype(BF16))
        vs = slice(h * GDN_DV, (h + 1) * GDN_DV)
        o_ref[:, vs] = (_rms(o, gain) * _silu(gate_ref[:, vs])).astype(BF16)

    @pl.when(c == pl.num_programs(1) - 1)
    def _():
        sfin_ref[0] = s_ref[...]


def _gdn(qh, kh, vh, bg, proj, onorm, s0, nb, nc, row_off):
    vv = GDN_HEADS * GDN_DV
    head = pl.BlockSpec((GDN_HEADS, CHUNK, GDN_DK), lambda b, c: (0, b * nc + c, 0))
    in_specs = [
        head, head, head,
        pl.BlockSpec((CHUNK, LANES), lambda b, c: (b * nc + c, 0)),
        pl.BlockSpec((CHUNK, vv), lambda b, c: (row_off + b * nc + c, GDN_CONV_CH // vv)),
        pl.BlockSpec((1, GDN_DV), lambda b, c: (0, 0)),
    ]
    args = [qh, kh, vh, bg, proj, onorm]
    state = pl.BlockSpec((1, GDN_HEADS, GDN_DK, GDN_DV), lambda b, c: (b, 0, 0, 0))
    if s0 is not None:
        in_specs.append(state)
        args.append(s0)
    return pl.pallas_call(
        functools.partial(_gdn_kernel, has_init=s0 is not None),
        out_shape=(jax.ShapeDtypeStruct((nb * nc * CHUNK, vv), BF16),
                   jax.ShapeDtypeStruct((nb, GDN_HEADS, GDN_DK, GDN_DV), F32)),
        grid=(nb, nc),
        in_specs=in_specs,
        out_specs=(pl.BlockSpec((CHUNK, vv), lambda b, c: (b * nc + c, 0)), state),
        scratch_shapes=[pltpu.VMEM((GDN_HEADS, GDN_DK, GDN_DV), F32)],
        compiler_params=_params("parallel", "arbitrary"),
        name="gdn",
    )(*args)


def _sb_block(q_m, k2, v2, upper, run, acc, mask):
    z = _dot_nt(q_m, k2) * (SB_DH ** -0.5)
    l1m = -_softplus(z)
    log_b = z + l1m
    if mask is not None:
        l1m = jnp.where(mask, l1m, 0.0)
    hi = l1m.astype(BF16)
    lo = (l1m - hi.astype(F32)).astype(BF16)
    surv = run + (_dot(hi, upper) + _dot(lo, upper))
    a = jnp.exp(log_b + surv)
    if mask is not None:
        a = jnp.where(mask, a, 0.0)
    acc = acc + _dot(a.astype(BF16), v2)
    run = run + jnp.sum(l1m, axis=-1, keepdims=True)
    return run, acc


def _sb_prompt_kernel(q_ref, k_ref, v_ref, o_ref):
    qi = pl.program_id(2)
    tq, tk = SB_TQ, SB_TK
    low = _iota2((tq, LANES), 1) < SB_DH
    upper = jnp.where(_iota2((tk, tk), 0) > _iota2((tk, tk), 1), 1.0, 0.0).astype(BF16)
    diag = _iota2((tq, tk), 1) < _iota2((tq, tk), 0)
    q2 = q_ref[...]
    outs = []
    for half in range(2):
        q_m = jnp.where(low, q2, 0) if half == 0 else jnp.where(low, 0, q2)
        r0 = pl.multiple_of(qi * tk, tk)
        run, acc = _sb_block(q_m, k_ref[pl.ds(r0, tk), :], v_ref[pl.ds(r0, tk), :], upper,
                             jnp.zeros((tq, 1), F32), jnp.zeros((tq, LANES), F32), diag)

        def body(i, carry):
            r = pl.multiple_of((qi - 1 - i) * tk, tk)
            return _sb_block(q_m, k_ref[pl.ds(r, tk), :], v_ref[pl.ds(r, tk), :], upper,
                             carry[0], carry[1], None)

        run, acc = lax.fori_loop(0, qi, body, (run, acc))
        outs.append(acc)
    o_ref[...] = jnp.where(low, outs[0], outs[1]).astype(BF16)


def _sb_prompt(pb, nb, seq):
    d = D_MODEL
    npair = d // LANES
    nq = seq // SB_TQ
    return pl.pallas_call(
        _sb_prompt_kernel,
        out_shape=jax.ShapeDtypeStruct((nb * seq, d), BF16),
        grid=(nb, npair, nq),
        in_specs=[
            pl.BlockSpec((SB_TQ, LANES), lambda b, p, i: (b * nq + i, p)),
            pl.BlockSpec((seq, LANES), lambda b, p, i: (b, npair + p)),
            pl.BlockSpec((seq, LANES), lambda b, p, i: (b, 2 * npair + p)),
        ],
        out_specs=pl.BlockSpec((SB_TQ, LANES), lambda b, p, i: (b * nq + i, p)),
        compiler_params=_params("parallel", "parallel", "arbitrary"),
        name="sb_prompt",
    )(pb, pb, pb)


def _sb_sample_kernel(q_ref, kn_ref, vn_ref, kc_ref, vc_ref, o_ref):
    tq, tk = CHUNK, SB_TK
    past = kc_ref.shape[1]
    low = _iota2((tq, LANES), 1) < SB_DH
    upper = jnp.where(_iota2((tk, tk), 0) > _iota2((tk, tk), 1), 1.0, 0.0).astype(BF16)
    upper_new = upper[0:tq, 0:tq]
    diag = _iota2((tq, tq), 1) < _iota2((tq, tq), 0)
    q2 = q_ref[...]
    outs = []
    for half in range(2):
        q_m = jnp.where(low, q2, 0) if half == 0 else jnp.where(low, 0, q2)
        run, acc = _sb_block(q_m, kn_ref[...], vn_ref[...], upper_new,
                             jnp.zeros((tq, 1), F32), jnp.zeros((tq, LANES), F32), diag)

        def body(i, carry):
            r = pl.multiple_of(past - (i + 1) * tk, tk)
            return _sb_block(q_m, kc_ref[0, pl.ds(r, tk), :].astype(BF16),
                             vc_ref[0, pl.ds(r, tk), :].astype(BF16), upper, carry[0], carry[1], None)

        run, acc = lax.fori_loop(0, past // tk, body, (run, acc))
        outs.append(acc)
    o_ref[...] = jnp.where(low, outs[0], outs[1]).astype(BF16)


def _sb_sample(pb, cache_k, cache_v, nb, row_off):
    d = D_MODEL
    npair = d // LANES
    past = cache_k.shape[1]
    new = lambda j: pl.BlockSpec((CHUNK, LANES), lambda b, p: (row_off + b, j * npair + p))
    old = pl.BlockSpec((1, past, LANES), lambda b, p: (b, 0, p))
    return pl.pallas_call(
        _sb_sample_kernel,
        out_shape=jax.ShapeDtypeStruct((nb * CHUNK, d), BF16),
        grid=(nb, npair),
        in_specs=[new(0), new(1), new(2), old, old],
        out_specs=pl.BlockSpec((CHUNK, LANES), lambda b, p: (b, p)),
        compiler_params=_params("parallel", "parallel"),
        name="sb_sample",
    )(pb, pb, pb, cache_k, cache_v)


def _pad_cols(w, n):
    return jnp.pad(w, ((0, 0), (0, n - w.shape[1])))


def kernel(x_prompt, x_sample, state_gla, cache_band_k, cache_band_v, state_gdn, state_gdn_conv,
           cache_sb_k, cache_sb_v, ffn1_norm, ffn1_w_gu, ffn1_w_down, mix_norm, ffn2_norm,
           ffn2_w_gu, ffn2_w_down, gla_w_in, gla_w_gate2, gla_b_gate, gla_onorm, gla_w_out,
           band_w_in, band_q_norm, band_k_norm, band_rel_bias, band_w_out, gdn_w_in, gdn_conv_w,
           gdn_a_log, gdn_dt_bias, gdn_onorm, gdn_w_out, sb_w_in, sb_w_out):
    bp, seq, d = x_prompt.shape
    bs, dec = x_sample.shape[:2]
    assert dec == CHUNK and seq % BAND_WINDOW == 0 and d == D_MODEL
    n_p = bp * seq
    n_s = bs * dec
    nc_p = seq // CHUNK
    off_s = n_p // CHUNK
    x = jnp.concatenate([x_prompt.reshape(n_p, d), x_sample.reshape(n_s, d)], axis=0)
    depth = ffn1_norm.shape[0]
    outs = {}
    for i in range(depth):
        m, j = i % 4, i // 4
        x = _ffn(x, ffn1_norm[i], ffn1_w_gu[i].astype(BF16), ffn1_w_down[i].astype(BF16))
        if m == 0:
            n_in = 2 * GLA_HEADS * GLA_DK + 2 * GLA_HEADS * GLA_DV + LANES
            proj, _ = _normproj(x, mix_norm[i], _pad_cols(gla_w_in[j], n_in).astype(BF16), n_in // 5)
            wg2 = jnp.pad(gla_w_gate2[j], ((0, LANES - GLA_RANK), (0, 0))).astype(BF16)
            common = (wg2, gla_b_gate[j].reshape(1, -1), gla_onorm[j].reshape(1, -1))
            o_p, s_p = _gla(proj, *common, None, bp, nc_p, 0)
            o_s, s_s = _gla(proj, *common, state_gla[j], bs, 1, off_s)
            outs.setdefault("gla", []).append((s_p, s_s))
            w_out = gla_w_out[j]
        elif m == 1:
            proj, _ = _normproj(x, mix_norm[i], band_w_in[j].astype(BF16), d)
            qn, kn, knb, vb = _band_prep(proj, band_q_norm[j], band_k_norm[j])
            rel = jnp.clip(BAND_WINDOW + jnp.arange(CHUNK)[:, None] - jnp.arange(BAND_SPAN)[None, :],
                           -REL_CLIP, REL_CLIP) + REL_CLIP
            bias = band_rel_bias[j][:, rel]
            o_p = _band_attn_prompt(qn, knb, vb, bias, bp, seq)
            o_s = _band_attn_sample(qn, knb, vb, cache_band_k[j].reshape(bs, BAND_WINDOW, d),
                                    cache_band_v[j].reshape(bs, BAND_WINDOW, d), bias, bs, off_s)
            keep = min(BAND_WINDOW, seq)
            shp = (BAND_HEADS, BAND_DH)
            v = proj[:, 2 * d:]
            outs.setdefault("band", []).append((
                kn[:n_p].reshape(bp, seq, *shp)[:, seq - keep:], v[:n_p].reshape(bp, seq, *shp)[:, seq - keep:],
                kn[n_p:].reshape(bs, dec, *shp), v[n_p:].reshape(bs, dec, *shp)))
            w_out = band_w_out[j]
        elif m == 2:
            n_in = GDN_CONV_CH + GDN_HEADS * GDN_DV + LANES
            proj, _ = _normproj(x, mix_norm[i], _pad_cols(gdn_w_in[j], n_in).astype(BF16), n_in // 3)
            par = jnp.zeros((2, LANES), F32)
            par = par.at[0, GDN_HEADS:2 * GDN_HEADS].set(-jnp.exp(gdn_a_log[j]))
            par = par.at[1, GDN_HEADS:2 * GDN_HEADS].set(gdn_dt_bias[j])
            c0 = jnp.pad(state_gdn_conv[j], ((0, 0), (SUBLANES - (CONV_W - 1), 0), (0, 0)))
            pre_p = _gdn_prep(proj, gdn_conv_w[j], par, None, bp, seq, 0, ROW_TILE)
            pre_s = _gdn_prep(proj, gdn_conv_w[j], par, c0, bs, dec, n_p, dec)
            onorm = gdn_onorm[j].reshape(1, -1)
            o_p, s_p = _gdn(*pre_p, proj, onorm, None, bp, nc_p, 0)
            o_s, s_s = _gdn(*pre_s, proj, onorm, state_gdn[j], bs, 1, off_s)
            qkv = proj[:, :GDN_CONV_CH]
            conv_p = qkv[:n_p].reshape(bp, seq, -1)[:, seq - (CONV_W - 1):]
            conv_s = qkv[n_p:].reshape(bs, dec, -1)[:, dec - (CONV_W - 1):]
            outs.setdefault("gdn", []).append((s_p, conv_p, s_s, conv_s))
            w_out = gdn_w_out[j]
        else:
            proj, pb = _normproj(x, mix_norm[i], sb_w_in[j].astype(BF16), d)
            o_p = _sb_prompt(pb, bp, seq)
            o_s = _sb_sample(pb, cache_sb_k[j].reshape(bs, -1, d), cache_sb_v[j].reshape(bs, -1, d), bs, off_s)
            shp = (SB_HEADS, SB_DH)
            k, v = proj[:, d:2 * d], proj[:, 2 * d:]
            outs.setdefault("sb", []).append((
                k[:n_p].reshape(bp, seq, *shp), v[:n_p].reshape(bp, seq, *shp),
                k[n_p:].reshape(bs, dec, *shp), v[n_p:].reshape(bs, dec, *shp)))
            w_out = sb_w_out[j]
        x = _outproj(x, jnp.concatenate([o_p, o_s], axis=0), w_out.astype(BF16))
        x = _ffn(x, ffn2_norm[i], ffn2_w_gu[i].astype(BF16), ffn2_w_down[i].astype(BF16))
    stack = lambda key, idx: jnp.stack([t[idx] for t in outs[key]])
    return (x[:n_p].reshape(bp, seq, d), x[n_p:].reshape(bs, dec, d),
            stack("gla", 0), stack("gla", 1),
            stack("band", 0), stack("band", 1), stack("band", 2), stack("band", 3),
            stack("gdn", 0), stack("gdn", 1), stack("gdn", 2), stack("gdn", 3),
            stack("sb", 0), stack("sb", 1), stack("sb", 2), stack("sb", 3))
```

```python
import functools

import jax
import jax.numpy as jnp
from jax import lax
from jax.experimental import pallas as pl
from jax.experimental.pallas import tpu as pltpu

F32 = jnp.float32
BF16 = jnp.bfloat16

D_MODEL = 1024
CHUNK = 64
EPS = 1e-6
D_FF = 2816
GLA_HEADS, GLA_DK, GLA_DV, GLA_RANK, GLA_TAU = 4, 128, 256, 16, 16.0
BAND_HEADS, BAND_DH, LEFT_CHUNKS, REL_CLIP = 16, 64, 8, 256
BAND_WINDOW = LEFT_CHUNKS * CHUNK
BAND_SPAN = BAND_WINDOW + CHUNK
GDN_HEADS, GDN_DK, GDN_DV, CONV_W = 8, 128, 128, 4
GDN_CONV_CH = 2 * GDN_HEADS * GDN_DK + GDN_HEADS * GDN_DV
SB_HEADS, SB_DH = 16, 64
LOG2E = 1.4426950408889634

LANES = 128
SUBLANES = 8
VMEM_LIMIT = 56 * 1024 * 1024
ROW_TILE = 512
FF_TILE = 1408
SB_TK = 256
SB_TQ = 2 * SB_TK


def _params(*sem):
    return pltpu.CompilerParams(dimension_semantics=sem, vmem_limit_bytes=VMEM_LIMIT)


def _dot(a, b):
    return jnp.dot(a, b, preferred_element_type=F32)


def _dot_nt(a, b):
    return lax.dot_general(a, b, (((1,), (1,)), ((), ())), preferred_element_type=F32)


def _dot_tn(a, b):
    return lax.dot_general(a, b, (((0,), (0,)), ((), ())), preferred_element_type=F32)


def _split3(x):
    h1 = x.astype(BF16)
    r1 = x - h1.astype(F32)
    h2 = r1.astype(BF16)
    h3 = (r1 - h2.astype(F32)).astype(BF16)
    return h1, h2, h3


def _tri_dot(tri, x):
    h1, h2, h3 = _split3(x)
    return _dot(tri, h1) + _dot(tri, h2) + _dot(tri, h3)


def _dot_f32(a, b):
    a1 = a.astype(BF16)
    a2 = (a - a1.astype(F32)).astype(BF16)
    b1 = b.astype(BF16)
    b2 = (b - b1.astype(F32)).astype(BF16)
    return _dot(a1, b1) + (_dot(a1, b2) + _dot(a2, b1))


def _sigmoid(x):
    return 1.0 / (1.0 + jnp.exp(-x))


def _silu(x):
    return x * _sigmoid(x)


def _softplus(x):
    return jnp.maximum(x, 0.0) + jnp.log1p(jnp.exp(-jnp.abs(x)))


def _log_sigmoid(x):
    return -_softplus(-x)


def _rms(x, g):
    return x * lax.rsqrt(jnp.mean(x * x, axis=-1, keepdims=True) + EPS) * g


def _iota2(shape, dim):
    return lax.broadcasted_iota(jnp.int32, shape, dim)


def _ffn_kernel(x_ref, g_ref, wg_ref, wu_ref, wd_ref, o_ref, xn_ref, acc_ref):
    j = pl.program_id(1)

    @pl.when(j == 0)
    def _():
        xn_ref[...] = _rms(x_ref[...], g_ref[...]).astype(BF16)
        acc_ref[...] = jnp.zeros_like(acc_ref)

    xn = xn_ref[...]
    g = _dot(xn, wg_ref[...])
    u = _dot(xn, wu_ref[...])
    h = (_silu(g) * u).astype(BF16)
    acc_ref[...] += _dot(h, wd_ref[...])

    @pl.when(j == pl.num_programs(1) - 1)
    def _():
        o_ref[...] = x_ref[...] + 0.5 * acc_ref[...]


def _ffn(x, gain, w_gu, w_down):
    t, d = x.shape
    nj = D_FF // FF_TILE
    return pl.pallas_call(
        _ffn_kernel,
        out_shape=jax.ShapeDtypeStruct((t, d), F32),
        grid=(t // ROW_TILE, nj),
        in_specs=[
            pl.BlockSpec((ROW_TILE, d), lambda i, j: (i, 0)),
            pl.BlockSpec((1, d), lambda i, j: (0, 0)),
            pl.BlockSpec((d, FF_TILE), lambda i, j: (0, j)),
            pl.BlockSpec((d, FF_TILE), lambda i, j: (0, j + nj)),
            pl.BlockSpec((FF_TILE, d), lambda i, j: (j, 0)),
        ],
        out_specs=pl.BlockSpec((ROW_TILE, d), lambda i, j: (i, 0)),
        scratch_shapes=[pltpu.VMEM((ROW_TILE, d), BF16), pltpu.VMEM((ROW_TILE, d), F32)],
        compiler_params=_params("parallel", "arbitrary"),
        name="ffn",
    )(x, gain.reshape(1, d), w_gu, w_gu, w_down)


def _normproj_kernel(x_ref, g_ref, w_ref, cs_ref, o_ref, ob_ref, xn_ref):
    @pl.when(pl.program_id(1) == 0)
    def _():
        xn_ref[...] = _rms(x_ref[...], g_ref[...]).astype(BF16)

    y = _dot(xn_ref[...], w_ref[...])
    o_ref[...] = y
    ob_ref[...] = (y * cs_ref[...]).astype(BF16)


def _normproj(x, gain, w, col_tile, col_scale=None):
    t, d = x.shape
    n = w.shape[1]
    if col_scale is None:
        col_scale = jnp.ones((n,), F32)
    return pl.pallas_call(
        _normproj_kernel,
        out_shape=(jax.ShapeDtypeStruct((t, n), F32), jax.ShapeDtypeStruct((t, n), BF16)),
        grid=(t // ROW_TILE, n // col_tile),
        in_specs=[
            pl.BlockSpec((ROW_TILE, d), lambda i, j: (i, 0)),
            pl.BlockSpec((1, d), lambda i, j: (0, 0)),
            pl.BlockSpec((d, col_tile), lambda i, j: (0, j)),
            pl.BlockSpec((1, col_tile), lambda i, j: (0, j)),
        ],
        out_specs=(pl.BlockSpec((ROW_TILE, col_tile), lambda i, j: (i, j)),
                   pl.BlockSpec((ROW_TILE, col_tile), lambda i, j: (i, j))),
        scratch_shapes=[pltpu.VMEM((ROW_TILE, d), BF16)],
        compiler_params=_params("parallel", "arbitrary"),
        name="normproj",
    )(x, gain.reshape(1, d), w, col_scale.reshape(1, n))


def _outproj_kernel(res_ref, a_ref, w_ref, o_ref):
    o_ref[...] = res_ref[...] + _dot(a_ref[...], w_ref[...])


def _outproj(res, a, w):
    t, d = res.shape
    return pl.pallas_call(
        _outproj_kernel,
        out_shape=jax.ShapeDtypeStruct((t, d), F32),
        grid=(t // ROW_TILE,),
        in_specs=[
            pl.BlockSpec((ROW_TILE, d), lambda i: (i, 0)),
            pl.BlockSpec((ROW_TILE, a.shape[1]), lambda i: (i, 0)),
            pl.BlockSpec(w.shape, lambda i: (0, 0)),
        ],
        out_specs=pl.BlockSpec((ROW_TILE, d), lambda i: (i, 0)),
        compiler_params=_params("parallel"),
        name="outproj",
    )(res, a, w)


def _gla_kernel(*refs, has_init):
    if has_init:
        (q_ref, k_ref, v_ref, r_ref, gl_ref, wg2_ref, bg_ref, on_ref, s0_ref,
         o_ref, sfin_ref, st_ref) = refs
    else:
        (q_ref, k_ref, v_ref, r_ref, gl_ref, wg2_ref, bg_ref, on_ref,
         o_ref, sfin_ref, st_ref) = refs
    c = pl.program_id(1)

    @pl.when(c == 0)
    def _():
        for h in range(GLA_HEADS):
            if has_init:
                st_ref[h] = s0_ref[0, h].T
            else:
                st_ref[h] = jnp.zeros((GLA_DV, GLA_DK), F32)

    incl = _iota2((CHUNK, CHUNK), 0) >= _iota2((CHUNK, CHUNK), 1)
    tri = jnp.where(incl, 1.0, 0.0).astype(BF16)
    x = _dot(gl_ref[...].astype(BF16), wg2_ref[...]) + bg_ref[...]
    log_a = _log_sigmoid(x) * (1.0 / GLA_TAU)
    b = _tri_dot(tri, log_a)
    b_last = b[CHUNK - 1:CHUNK, :]
    q = q_ref[...] * (GLA_DK ** -0.5)
    k = k_ref[...]
    q_e = (q * jnp.exp(b)).astype(BF16)
    k_e = (k * jnp.exp(-b)).astype(BF16)
    k_l = (k * jnp.exp(b_last - b)).astype(BF16)
    d_last = jnp.exp(b_last)
    gain = on_ref[...]
    for h in range(GLA_HEADS):
        ks = slice(h * GLA_DK, (h + 1) * GLA_DK)
        vs = slice(h * GLA_DV, (h + 1) * GLA_DV)
        v = v_ref[:, vs].astype(BF16)
        st = st_ref[h]
        att = jnp.where(incl, _dot_nt(q_e[:, ks], k_e[:, ks]), 0.0)
        o = _dot(att.astype(BF16), v) + _dot_nt(q_e[:, ks], st.astype(BF16))
        st_ref[h] = st * d_last[:, ks] + _dot_tn(v, k_l[:, ks])
        o = _rms(o, gain) * _silu(r_ref[:, vs])
        o_ref[:, vs] = o.astype(BF16)

    @pl.when(c == pl.num_programs(1) - 1)
    def _():
        for h in range(GLA_HEADS):
            sfin_ref[0, h] = st_ref[h].T


def _gla(proj, w_gate2, b_gate, onorm, s0, nb, nc, row_off):
    qk = GLA_HEADS * GLA_DK
    vv = GLA_HEADS * GLA_DV
    row = lambda b, c: row_off + b * nc + c
    in_specs = [
        pl.BlockSpec((CHUNK, qk), lambda b, c: (row(b, c), 0)),
        pl.BlockSpec((CHUNK, qk), lambda b, c: (row(b, c), 1)),
        pl.BlockSpec((CHUNK, vv), lambda b, c: (row(b, c), 1)),
        pl.BlockSpec((CHUNK, vv), lambda b, c: (row(b, c), 2)),
        pl.BlockSpec((CHUNK, LANES), lambda b, c: (row(b, c), (2 * qk + 2 * vv) // LANES)),
        pl.BlockSpec((LANES, qk), lambda b, c: (0, 0)),
        pl.BlockSpec((1, qk), lambda b, c: (0, 0)),
        pl.BlockSpec((1, GLA_DV), lambda b, c: (0, 0)),
    ]
    args = [proj, proj, proj, proj, proj, w_gate2, b_gate, onorm]
    if s0 is not None:
        in_specs.append(pl.BlockSpec((1, GLA_HEADS, GLA_DK, GLA_DV), lambda b, c: (b, 0, 0, 0)))
        args.append(s0)
    return pl.pallas_call(
        functools.partial(_gla_kernel, has_init=s0 is not None),
        out_shape=(jax.ShapeDtypeStruct((nb * nc * CHUNK, vv), BF16),
                   jax.ShapeDtypeStruct((nb, GLA_HEADS, GLA_DK, GLA_DV), F32)),
        grid=(nb, nc),
        in_specs=in_specs,
        out_specs=(pl.BlockSpec((CHUNK, vv), lambda b, c: (b * nc + c, 0)),
                   pl.BlockSpec((1, GLA_HEADS, GLA_DK, GLA_DV), lambda b, c: (b, 0, 0, 0))),
        scratch_shapes=[pltpu.VMEM((GLA_HEADS, GLA_DV, GLA_DK), F32)],
        compiler_params=_params("parallel", "arbitrary"),
        name="gla",
    )(*args)


def _pair_head_norm(x, gain2):
    low = _iota2(x.shape, 1) < BAND_DH
    sq = x * x
    s_lo = jnp.sum(jnp.where(low, sq, 0.0), axis=-1, keepdims=True)
    s_hi = jnp.sum(jnp.where(low, 0.0, sq), axis=-1, keepdims=True)
    ms = jnp.where(low, s_lo, s_hi) * (1.0 / BAND_DH)
    return x * lax.rsqrt(ms + EPS) * gain2


def _band_prep_kernel(q_ref, k_ref, v_ref, qg_ref, kg_ref, qn_ref, kn_ref, knb_ref, vb_ref):
    for p in range(D_MODEL // LANES):
        ls = slice(p * LANES, (p + 1) * LANES)
        qn_ref[:, ls] = _pair_head_norm(q_ref[:, ls], qg_ref[...]).astype(BF16)
        kn = _pair_head_norm(k_ref[:, ls], kg_ref[...])
        kn_ref[:, ls] = kn
        knb_ref[:, ls] = kn.astype(BF16)
    vb_ref[...] = v_ref[...].astype(BF16)


def _band_prep(proj, q_gain, k_gain):
    t = proj.shape[0]
    d = D_MODEL
    col = lambda j: pl.BlockSpec((ROW_TILE, d), lambda i: (i, j))
    vec = pl.BlockSpec((1, LANES), lambda i: (0, 0))
    out = pl.BlockSpec((ROW_TILE, d), lambda i: (i, 0))
    return pl.pallas_call(
        _band_prep_kernel,
        out_shape=(jax.ShapeDtypeStruct((t, d), BF16), jax.ShapeDtypeStruct((t, d), F32),
                   jax.ShapeDtypeStruct((t, d), BF16), jax.ShapeDtypeStruct((t, d), BF16)),
        grid=(t // ROW_TILE,),
        in_specs=[col(0), col(1), col(2), vec, vec],
        out_specs=(out, out, out, out),
        compiler_params=_params("parallel"),
        name="band_prep",
    )(proj, proj, proj, jnp.tile(q_gain, 2).reshape(1, LANES), jnp.tile(k_gain, 2).reshape(1, LANES))


def _band_attn_kernel(q_ref, kp_ref, kc_ref, vp_ref, vc_ref, bias_ref, o_ref, kall_ref, vall_ref,
                      *, nq, prompt):
    if prompt:
        kall_ref[0:BAND_WINDOW] = kp_ref[...]
        vall_ref[0:BAND_WINDOW] = vp_ref[...]
    else:
        kall_ref[0:BAND_WINDOW] = kp_ref[0].astype(BF16)
        vall_ref[0:BAND_WINDOW] = vp_ref[0].astype(BF16)
    kall_ref[BAND_WINDOW:] = kc_ref[...]
    vall_ref[BAND_WINDOW:] = vc_ref[...]
    first_pos = (pl.program_id(1) - 1) * BAND_WINDOW if prompt else 0
    key_idx = _iota2((2 * CHUNK, BAND_SPAN), 1)
    pairs = range(D_MODEL // LANES)
    lanes = [slice(p * LANES, (p + 1) * LANES) for p in pairs]

    def chunk(t, carry):
        r0 = pl.multiple_of(t * CHUNK, CHUNK)
        qs = [_stack_pair(q_ref[pl.ds(r0, CHUNK), ls]) for ls in lanes]
        s = [_dot_nt(qs[p], kall_ref[pl.ds(r0, BAND_SPAN), lanes[p]]) * (BAND_DH ** -0.5) + bias_ref[p]
             for p in pairs]
        if prompt:
            visible = first_pos + r0 + key_idx >= 0
            s = [jnp.where(visible, x, -jnp.inf) for x in s]
        e = [jnp.exp(x - jnp.max(x, axis=-1, keepdims=True)) for x in s]
        prob = [(x / jnp.sum(x, axis=-1, keepdims=True)).astype(BF16) for x in e]
        pv = [_dot(prob[p], vall_ref[pl.ds(r0, BAND_SPAN), lanes[p]]) for p in pairs]
        for p in pairs:
            o_ref[pl.ds(r0, CHUNK), lanes[p]] = _unstack_pair(pv[p]).astype(BF16)
        return carry

    lax.fori_loop(0, nq, chunk, 0)


def _band_attn_prompt(qn, knb, vb, bias, nb, seq):
    d = D_MODEL
    blk = BAND_WINDOW
    nblk = seq // blk
    cur = pl.BlockSpec((blk, d), lambda b, i: (b * nblk + i, 0))
    prev = pl.BlockSpec((blk, d), lambda b, i: (b * nblk + jnp.maximum(i - 1, 0), 0))
    return pl.pallas_call(
        functools.partial(_band_attn_kernel, nq=blk // CHUNK, prompt=True),
        out_shape=jax.ShapeDtypeStruct((nb * seq, d), BF16),
        grid=(nb, nblk),
        in_specs=[cur, prev, cur, prev, cur,
                  pl.BlockSpec(bias.shape, lambda b, i: (0, 0, 0))],
        out_specs=cur,
        scratch_shapes=[pltpu.VMEM((2 * blk, d), BF16), pltpu.VMEM((2 * blk, d), BF16)],
        compiler_params=_params("parallel", "arbitrary"),
        name="band_attn_prompt",
    )(qn, knb, knb, vb, vb, bias)


def _band_attn_sample(qn, knb, vb, cache_k, cache_v, bias, nb, row_off):
    d = D_MODEL
    new = pl.BlockSpec((CHUNK, d), lambda b: (row_off + b, 0))
    old = pl.BlockSpec((1, BAND_WINDOW, d), lambda b: (b, 0, 0))
    return pl.pallas_call(
        functools.partial(_band_attn_kernel, nq=1, prompt=False),
        out_shape=jax.ShapeDtypeStruct((nb * CHUNK, d), BF16),
        grid=(nb,),
        in_specs=[new, old, new, old, new, pl.BlockSpec(bias.shape, lambda b: (0, 0, 0))],
        out_specs=pl.BlockSpec((CHUNK, d), lambda b: (b, 0)),
        scratch_shapes=[pltpu.VMEM((BAND_SPAN, d), BF16), pltpu.VMEM((BAND_SPAN, d), BF16)],
        compiler_params=_params("parallel"),
        name="band_attn_sample",
    )(qn, cache_k, knb, cache_v, vb, bias)


def _gdn_prep_kernel(*refs, has_init):
    if has_init:
        x_ref, raw_ref, cw_ref, par_ref, c0_ref, q_ref, k_ref, v_ref, bg_ref, carry_ref = refs
    else:
        x_ref, raw_ref, cw_ref, par_ref, q_ref, k_ref, v_ref, bg_ref, carry_ref = refs
    tb = x_ref.shape[0]

    @pl.when(pl.program_id(1) == 0)
    def _():
        if has_init:
            carry_ref[...] = c0_ref[0]
        else:
            carry_ref[...] = jnp.zeros_like(carry_ref)

    x = x_ref[...]
    w = cw_ref[...]
    x8 = x[0:SUBLANES]
    c8 = carry_ref[...]
    row8 = _iota2(x8.shape, 0)
    y = x * w[CONV_W - 1:CONV_W]
    y8 = x8 * w[CONV_W - 1:CONV_W]
    for s in range(1, CONV_W):
        ws = w[CONV_W - 1 - s:CONV_W - s]
        y = y + pltpu.roll(x, s, 0) * ws
        y8 = y8 + jnp.where(row8 < s, pltpu.roll(c8, s, 0), pltpu.roll(x8, s, 0)) * ws
    carry_ref[...] = x[tb - SUBLANES:tb]
    if tb > SUBLANES:
        y = jnp.concatenate([y8, y[SUBLANES:]], axis=0)
    else:
        y = y8
    y = _silu(y)
    qk = GDN_HEADS * GDN_DK
    for h in range(GDN_HEADS):
        qh = y[:, h * GDN_DK:(h + 1) * GDN_DK]
        kh = y[:, qk + h * GDN_DK:qk + (h + 1) * GDN_DK]
        q_ref[h] = qh * lax.rsqrt(jnp.sum(qh * qh, axis=-1, keepdims=True) + EPS)
        k_ref[h] = kh * lax.rsqrt(jnp.sum(kh * kh, axis=-1, keepdims=True) + EPS)
        v_ref[h] = y[:, 2 * qk + h * GDN_DV:2 * qk + (h + 1) * GDN_DV]
    raw = raw_ref[...]
    beta = _sigmoid(raw)
    g = par_ref[0:1, :] * _softplus(raw + par_ref[1:2, :])
    bg_ref[...] = jnp.where(_iota2(raw.shape, 1) < GDN_HEADS, beta, g)


def _gdn_prep(proj, conv_w, par, c0, nb, seq, row_off, tb):
    nblk = seq // tb
    base = row_off // tb
    row = lambda b, i: base + b * nblk + i
    in_specs = [
        pl.BlockSpec((tb, GDN_CONV_CH), lambda b, i: (row(b, i), 0)),
        pl.BlockSpec((tb, LANES), lambda b, i: (row(b, i), (GDN_CONV_CH + GDN_HEADS * GDN_DV) // LANES)),
        pl.BlockSpec((CONV_W, GDN_CONV_CH), lambda b, i: (0, 0)),
        pl.BlockSpec((2, LANES), lambda b, i: (0, 0)),
    ]
    args = [proj, proj, conv_w, par]
    if c0 is not None:
        in_specs.append(pl.BlockSpec((1, SUBLANES, GDN_CONV_CH), lambda b, i: (b, 0, 0)))
        args.append(c0)
    n = nb * seq
    head = pl.BlockSpec((GDN_HEADS, tb, GDN_DK), lambda b, i: (0, b * nblk + i, 0))
    return pl.pallas_call(
        functools.partial(_gdn_prep_kernel, has_init=c0 is not None),
        out_shape=(jax.ShapeDtypeStruct((GDN_HEADS, n, GDN_DK), F32),
                   jax.ShapeDtypeStruct((GDN_HEADS, n, GDN_DK), F32),
                   jax.ShapeDtypeStruct((GDN_HEADS, n, GDN_DV), F32),
                   jax.ShapeDtypeStruct((n, LANES), F32)),
        grid=(nb, nblk),
        in_specs=in_specs,
        out_specs=(head, head, head, pl.BlockSpec((tb, LANES), lambda b, i: (b * nblk + i, 0))),
        scratch_shapes=[pltpu.VMEM((SUBLANES, GDN_CONV_CH), F32)],
        compiler_params=_params("parallel", "arbitrary"),
        name="gdn_prep",
    )(*args)


def _gdn_kernel(*refs, has_init):
    if has_init:
        q_ref, k_ref, v_ref, bg_ref, gate_ref, on_ref, s0_ref, o_ref, sfin_ref, s_ref = refs
    else:
        q_ref, k_ref, v_ref, bg_ref, gate_ref, on_ref, o_ref, sfin_ref, s_ref = refs
    c = pl.program_id(1)

    @pl.when(c == 0)
    def _():
        if has_init:
            s_ref[...] = s0_ref[0]
        else:
            s_ref[...] = jnp.zeros_like(s_ref)

    ri = _iota2((CHUNK, CHUNK), 0)
    ci = _iota2((CHUNK, CHUNK), 1)
    incl = ri >= ci
    strict = ri > ci
    tri = jnp.where(incl, 1.0, 0.0).astype(BF16)
    eye = jnp.where(ri == ci, 1.0, 0.0)
    bg = bg_ref[...]
    cs = _tri_dot(tri, bg)
    cs_t = cs.T
    gain = on_ref[...]
    heads = range(GDN_HEADS)
    beta = [bg[:, h:h + 1] for h in heads]
    gcol = [cs[:, GDN_HEADS + h:GDN_HEADS + h + 1] for h in heads]
    glast = [g[CHUNK - 1:CHUNK, :] for g in gcol]
    gam = [jnp.exp(jnp.where(incl, gcol[h] - cs_t[GDN_HEADS + h:GDN_HEADS + h + 1, :], -jnp.inf)) for h in heads]
    egc = [jnp.exp(g) for g in gcol]
    k = [k_ref[h] for h in heads]
    k16 = [x.astype(BF16) for x in k]
    kb = [k[h] * beta[h] for h in heads]
    m = [jnp.where(strict, _dot_nt(kb[h].astype(BF16), k16[h]) * gam[h], 0.0) for h in heads]
    d = [eye] * GDN_HEADS
    for lvl in range(6):
        sub = (((ri >> lvl) & 1) == 1) & (((ci >> lvl) & 1) == 0) & ((ri >> (lvl + 1)) == (ci >> (lvl + 1)))
        y = [_dot_f32(d[h], jnp.where(sub, m[h], 0.0)) for h in heads]
        x = [_dot_f32(y[h], d[h]) for h in heads]
        d = [d[h] - x[h] for h in heads]
    t16 = [t.astype(BF16) for t in d]
    u = [_dot(t16[h], (v_ref[h] * beta[h]).astype(BF16)) for h in heads]
    w = [_dot(t16[h], (kb[h] * egc[h]).astype(BF16)) for h in heads]
    q = [q_ref[h] * (GDN_DK ** -0.5) for h in heads]
    att = [_dot_nt(q[h].astype(BF16), k16[h]) * gam[h] for h in heads]
    q_e = [(q[h] * egc[h]).astype(BF16) for h in heads]
    k_l = [(k[h] * jnp.exp(glast[h] - gcol[h])).astype(BF16) for h in heads]
    s = [s_ref[h] for h in heads]
    s16 = [x.astype(BF16) for x in s]
    v_new = [u[h] - _dot(w[h].astype(BF16), s16[h]) for h in heads]
    v16 = [x.astype(BF16) for x in v_new]
    o = [_dot(q_e[h], s16[h]) + _dot(att[h].astype(BF16), v16[h]) for h in heads]
    for h in heads:
        s_ref[h] = s[h] * jnp.exp(glast[h]) + _dot_tn(k_l[h], v16[h])
        vs = slice(h * GDN_DV, (h + 1) * GDN_DV)
        o_ref[:, vs] = (_rms(o[h], gain) * _silu(gate_ref[:, vs])).astype(BF16)

    @pl.when(c == pl.num_programs(1) - 1)
    def _():
        sfin_ref[0] = s_ref[...]


def _gdn(qh, kh, vh, bg, proj, onorm, s0, nb, nc, row_off):
    vv = GDN_HEADS * GDN_DV
    head = pl.BlockSpec((GDN_HEADS, CHUNK, GDN_DK), lambda b, c: (0, b * nc + c, 0))
    in_specs = [
        head, head, head,
        pl.BlockSpec((CHUNK, LANES), lambda b, c: (b * nc + c, 0)),
        pl.BlockSpec((CHUNK, vv), lambda b, c: (row_off + b * nc + c, GDN_CONV_CH // vv)),
        pl.BlockSpec((1, GDN_DV), lambda b, c: (0, 0)),
    ]
    args = [qh, kh, vh, bg, proj, onorm]
    state = pl.BlockSpec((1, GDN_HEADS, GDN_DK, GDN_DV), lambda b, c: (b, 0, 0, 0))
    if s0 is not None:
        in_specs.append(state)
        args.append(s0)
    return pl.pallas_call(
        functools.partial(_gdn_kernel, has_init=s0 is not None),
        out_shape=(jax.ShapeDtypeStruct((nb * nc * CHUNK, vv), BF16),
                   jax.ShapeDtypeStruct((nb, GDN_HEADS, GDN_DK, GDN_DV), F32)),
        grid=(nb, nc),
        in_specs=in_specs,
        out_specs=(pl.BlockSpec((CHUNK, vv), lambda b, c: (b * nc + c, 0)), state),
        scratch_shapes=[pltpu.VMEM((GDN_HEADS, GDN_DK, GDN_DV), F32)],
        compiler_params=_params("parallel", "arbitrary"),
        name="gdn",
    )(*args)


def _neg_abs(x):
    bits = lax.bitcast_convert_type(x, jnp.uint32) | jnp.uint32(0x80000000)
    return lax.bitcast_convert_type(bits, F32)


def _stack_pair(q2):
    low = _iota2(q2.shape, 1) < SB_DH
    return jnp.concatenate([jnp.where(low, q2, 0), jnp.where(low, 0, q2)], axis=0)


def _unstack_pair(acc):
    tq = acc.shape[0] // 2
    low = _iota2((tq, LANES), 1) < SB_DH
    return jnp.where(low, acc[:tq], acc[tq:])


def _upper2(tk):
    r = _iota2((2 * tk, tk), 0)
    r = jnp.where(r >= tk, r - tk, r)
    return jnp.where(r > _iota2((2 * tk, tk), 1), 1.0, 0.0).astype(BF16)


def _diag_mask(tq, tk):
    r = _iota2((2 * tq, tk), 0)
    r = jnp.where(r >= tq, r - tq, r)
    return _iota2((2 * tq, tk), 1) < r


def _sb_step(qss, kv, upper2, carries, masks):
    k2, v2 = kv
    sa = [_sb_scores(qs, k2, mask) for qs, mask in zip(qss, masks)]
    sb = [_sb_cumsum(a, upper2) for a in sa]
    return tuple(_sb_apply(b, v2, c, mask) for b, c, mask in zip(sb, carries, masks))


def _sb_scores(qs, k2, mask):
    zn = _dot_nt(qs, k2)
    l1m = jnp.minimum(zn, 0.0) - jnp.log2(1.0 + jnp.exp2(_neg_abs(zn)))
    log_b = l1m - zn
    if mask is not None:
        l1m = jnp.where(mask, l1m, 0.0)
    hi = l1m.astype(BF16)
    lo = (l1m - hi.astype(F32)).astype(BF16)
    return jnp.concatenate([hi, lo], axis=1), log_b, jnp.sum(l1m, axis=-1, keepdims=True)


def _sb_cumsum(sa, upper2):
    hilo, log_b, tot = sa
    return _dot(hilo, upper2), log_b, tot


def _sb_apply(sb, v2, carry, mask):
    cs, log_b, tot = sb
    run, acc = carry
    a = jnp.exp2(log_b + (cs + run))
    if mask is not None:
        a = jnp.where(mask, a, 0.0)
    return run + tot, acc + _dot(a.astype(BF16), v2)


def _sb_init(tq):
    return jnp.zeros((2 * tq, 1), F32), jnp.zeros((2 * tq, LANES), F32)


SB_NEUTRAL = 1e30


def _sb_pipelined(qss, k_of, v_of, top, npairs, upper2, scratch, carries):
    z_ref, hilo_ref, logb_ref, tot_ref, cs_ref, a_ref, run_ref, acc_ref = scratch
    n = len(qss)

    def half(h, p):
        for c in range(n):
            acc_ref[c] += _dot(a_ref[2 * c + 1 - p], v_of(jnp.minimum(top - h + 4, top)))
        for c in range(n):
            run = run_ref[c]
            a_ref[2 * c + p] = jnp.exp2(cs_ref[2 * c + 1 - p] + (logb_ref[2 * c + p] + run)).astype(BF16)
            run_ref[c] = run + tot_ref[2 * c + p]
        for c in range(n):
            cs_ref[2 * c + p] = _dot(hilo_ref[2 * c + 1 - p], upper2)
        for c in range(n):
            zn = z_ref[2 * c + 1 - p]
            l1m = jnp.minimum(zn, 0.0) - jnp.log2(1.0 + jnp.exp2(_neg_abs(zn)))
            logb_ref[2 * c + p] = l1m - zn
            hi = l1m.astype(BF16)
            hilo_ref[2 * c + p] = jnp.concatenate([hi, (l1m - hi.astype(F32)).astype(BF16)], axis=1)
            tot_ref[2 * c + p] = jnp.sum(l1m, axis=-1, keepdims=True)
        k2 = k_of(top - h)
        for c, qs in enumerate(qss):
            z_ref[2 * c + p] = _dot_nt(qs, k2)

    logb_ref[...] = jnp.full(logb_ref.shape, -SB_NEUTRAL, F32)
    tot_ref[...] = jnp.zeros(tot_ref.shape, F32)
    for c, (run, acc) in enumerate(carries):
        z_ref[2 * c + 1] = jnp.full(z_ref.shape[1:], SB_NEUTRAL, F32)
        hilo_ref[2 * c + 1] = jnp.zeros(hilo_ref.shape[1:], BF16)
        cs_ref[2 * c + 1] = jnp.zeros(cs_ref.shape[1:], F32)
        a_ref[2 * c + 1] = jnp.zeros(a_ref.shape[1:], BF16)
        run_ref[c] = run
        acc_ref[c] = acc

    def body(i, _):
        half(2 * i, 0)
        half(2 * i + 1, 1)
        return 0

    lax.fori_loop(0, npairs + 2, body, 0)
    return [acc_ref[c] for c in range(n)]


def _sb_scratch(nchain, rows, tk):
    slots = 2 * nchain
    return [pltpu.VMEM((slots, rows, tk), F32), pltpu.VMEM((slots, rows, 2 * tk), BF16),
            pltpu.VMEM((slots, rows, tk), F32), pltpu.VMEM((slots, rows, 1), F32),
            pltpu.VMEM((slots, rows, tk), F32), pltpu.VMEM((slots, rows, tk), BF16),
            pltpu.VMEM((nchain, rows, 1), F32), pltpu.VMEM((nchain, rows, LANES), F32)]


def _sb_prompt_kernel(q_ref, k_ref, v_ref, o_ref, *scratch):
    tk = SB_TK
    qi = pl.program_id(2)
    kb = 2 * qi
    upper2 = _upper2(tk)
    qss = [_stack_pair(q_ref[0:tk, :]), _stack_pair(q_ref[tk:2 * tk, :])]
    diag = _diag_mask(tk, tk)

    def rows(blk):
        return pl.ds(pl.multiple_of(jnp.maximum(blk, 0) * tk, tk), tk)

    k_of = lambda blk: k_ref[rows(blk), :]
    v_of = lambda blk: v_ref[rows(blk), :]
    (cy,) = _sb_step(qss[1:], (k_of(kb + 1), v_of(kb + 1)), upper2, [_sb_init(tk)], [diag])
    carry = _sb_step(qss, (k_of(kb), v_of(kb)), upper2, [_sb_init(tk), cy], [diag, None])
    accs = _sb_pipelined(qss, k_of, v_of, kb - 1, qi, upper2, scratch, carry)
    o_ref[...] = jnp.concatenate([_unstack_pair(acc) for acc in accs], axis=0).astype(BF16)


def _sb_prompt(pb, nb, seq):
    d = D_MODEL
    npair = d // LANES
    nq = seq // SB_TQ
    return pl.pallas_call(
        _sb_prompt_kernel,
        out_shape=jax.ShapeDtypeStruct((nb * seq, d), BF16),
        grid=(nb, npair, nq),
        in_specs=[
            pl.BlockSpec((SB_TQ, LANES), lambda b, p, i: (b * nq + i, p)),
            pl.BlockSpec((seq, LANES), lambda b, p, i: (b, npair + p)),
            pl.BlockSpec((seq, LANES), lambda b, p, i: (b, 2 * npair + p)),
        ],
        out_specs=pl.BlockSpec((SB_TQ, LANES), lambda b, p, i: (b * nq + i, p)),
        scratch_shapes=_sb_scratch(2, 2 * SB_TK, SB_TK),
        compiler_params=_params("parallel", "parallel", "arbitrary"),
        name="sb_prompt",
    )(pb, pb, pb)


def _sb_sample_kernel(q_ref, kn_ref, vn_ref, kc_ref, vc_ref, o_ref, *scratch):
    tq, tk = CHUNK, SB_TK
    past = kc_ref.shape[1]
    qs = _stack_pair(q_ref[...])
    carry = _sb_step([qs], (kn_ref[...], vn_ref[...]), _upper2(tq), [_sb_init(tq)], [_diag_mask(tq, tq)])

    def rows(blk):
        return pl.ds(pl.multiple_of(jnp.maximum(blk, 0) * tk, tk), tk)

    nblk = past // tk
    assert nblk % 2 == 0
    (acc,) = _sb_pipelined([qs], lambda blk: kc_ref[0, rows(blk), :].astype(BF16),
                           lambda blk: vc_ref[0, rows(blk), :].astype(BF16),
                           nblk - 1, nblk // 2, _upper2(tk), scratch, carry)
    o_ref[...] = _unstack_pair(acc).astype(BF16)


def _sb_sample(pb, cache_k, cache_v, nb, row_off):
    d = D_MODEL
    npair = d // LANES
    past = cache_k.shape[1]
    new = lambda j: pl.BlockSpec((CHUNK, LANES), lambda b, p: (row_off + b, j * npair + p))
    old = pl.BlockSpec((1, past, LANES), lambda b, p: (b, 0, p))
    return pl.pallas_call(
        _sb_sample_kernel,
        out_shape=jax.ShapeDtypeStruct((nb * CHUNK, d), BF16),
        grid=(nb, npair),
        in_specs=[new(0), new(1), new(2), old, old],
        out_specs=pl.BlockSpec((CHUNK, LANES), lambda b, p: (b, p)),
        scratch_shapes=_sb_scratch(1, 2 * CHUNK, SB_TK),
        compiler_params=_params("parallel", "parallel"),
        name="sb_sample",
    )(pb, pb, pb, cache_k, cache_v)


def _pad_cols(w, n):
    return jnp.pad(w, ((0, 0), (0, n - w.shape[1])))


def kernel(x_prompt, x_sample, state_gla, cache_band_k, cache_band_v, state_gdn, state_gdn_conv,
           cache_sb_k, cache_sb_v, ffn1_norm, ffn1_w_gu, ffn1_w_down, mix_norm, ffn2_norm,
           ffn2_w_gu, ffn2_w_down, gla_w_in, gla_w_gate2, gla_b_gate, gla_onorm, gla_w_out,
           band_w_in, band_q_norm, band_k_norm, band_rel_bias, band_w_out, gdn_w_in, gdn_conv_w,
           gdn_a_log, gdn_dt_bias, gdn_onorm, gdn_w_out, sb_w_in, sb_w_out):
    bp, seq, d = x_prompt.shape
    bs, dec = x_sample.shape[:2]
    assert dec == CHUNK and seq % BAND_WINDOW == 0 and d == D_MODEL
    n_p = bp * seq
    n_s = bs * dec
    nc_p = seq // CHUNK
    off_s = n_p // CHUNK
    x = jnp.concatenate([x_prompt.reshape(n_p, d), x_sample.reshape(n_s, d)], axis=0)
    depth = ffn1_norm.shape[0]
    outs = {}
    for i in range(depth):
        m, j = i % 4, i // 4
        x = _ffn(x, ffn1_norm[i], ffn1_w_gu[i].astype(BF16), ffn1_w_down[i].astype(BF16))
        if m == 0:
            n_in = 2 * GLA_HEADS * GLA_DK + 2 * GLA_HEADS * GLA_DV + LANES
            proj, _ = _normproj(x, mix_norm[i], _pad_cols(gla_w_in[j], n_in).astype(BF16), n_in // 5)
            wg2 = jnp.pad(gla_w_gate2[j], ((0, LANES - GLA_RANK), (0, 0))).astype(BF16)
            common = (wg2, gla_b_gate[j].reshape(1, -1), gla_onorm[j].reshape(1, -1))
            o_p, s_p = _gla(proj, *common, None, bp, nc_p, 0)
            o_s, s_s = _gla(proj, *common, state_gla[j], bs, 1, off_s)
            outs.setdefault("gla", []).append((s_p, s_s))
            w_out = gla_w_out[j]
        elif m == 1:
            proj, _ = _normproj(x, mix_norm[i], band_w_in[j].astype(BF16), d)
            qn, kn, knb, vb = _band_prep(proj, band_q_norm[j], band_k_norm[j])
            rel = jnp.clip(BAND_WINDOW + jnp.arange(CHUNK)[:, None] - jnp.arange(BAND_SPAN)[None, :],
                           -REL_CLIP, REL_CLIP) + REL_CLIP
            bias = band_rel_bias[j][:, rel].reshape(BAND_HEADS // 2, 2 * CHUNK, BAND_SPAN)
            o_p = _band_attn_prompt(qn, knb, vb, bias, bp, seq)
            o_s = _band_attn_sample(qn, knb, vb, cache_band_k[j].reshape(bs, BAND_WINDOW, d),
                                    cache_band_v[j].reshape(bs, BAND_WINDOW, d), bias, bs, off_s)
            keep = min(BAND_WINDOW, seq)
            shp = (BAND_HEADS, BAND_DH)
            v = proj[:, 2 * d:]
            outs.setdefault("band", []).append((
                kn[:n_p].reshape(bp, seq, *shp)[:, seq - keep:], v[:n_p].reshape(bp, seq, *shp)[:, seq - keep:],
                kn[n_p:].reshape(bs, dec, *shp), v[n_p:].reshape(bs, dec, *shp)))
            w_out = band_w_out[j]
        elif m == 2:
            n_in = GDN_CONV_CH + GDN_HEADS * GDN_DV + LANES
            proj, _ = _normproj(x, mix_norm[i], _pad_cols(gdn_w_in[j], n_in).astype(BF16), n_in // 3)
            par = jnp.zeros((2, LANES), F32)
            par = par.at[0, GDN_HEADS:2 * GDN_HEADS].set(-jnp.exp(gdn_a_log[j]))
            par = par.at[1, GDN_HEADS:2 * GDN_HEADS].set(gdn_dt_bias[j])
            c0 = jnp.pad(state_gdn_conv[j], ((0, 0), (SUBLANES - (CONV_W - 1), 0), (0, 0)))
            pre_p = _gdn_prep(proj, gdn_conv_w[j], par, None, bp, seq, 0, ROW_TILE)
            pre_s = _gdn_prep(proj, gdn_conv_w[j], par, c0, bs, dec, n_p, dec)
            onorm = gdn_onorm[j].reshape(1, -1)
            o_p, s_p = _gdn(*pre_p, proj, onorm, None, bp, nc_p, 0)
            o_s, s_s = _gdn(*pre_s, proj, onorm, state_gdn[j], bs, 1, off_s)
            qkv = proj[:, :GDN_CONV_CH]
            conv_p = qkv[:n_p].reshape(bp, seq, -1)[:, seq - (CONV_W - 1):]
            conv_s = qkv[n_p:].reshape(bs, dec, -1)[:, dec - (CONV_W - 1):]
            outs.setdefault("gdn", []).append((s_p, conv_p, s_s, conv_s))
            w_out = gdn_w_out[j]
        else:
            q_scale = jnp.where(jnp.arange(3 * d) < d, -LOG2E * SB_DH ** -0.5, 1.0).astype(F32)
            proj, pb = _normproj(x, mix_norm[i], sb_w_in[j].astype(BF16), d, q_scale)
            o_p = _sb_prompt(pb, bp, seq)
            o_s = _sb_sample(pb, cache_sb_k[j].reshape(bs, -1, d), cache_sb_v[j].reshape(bs, -1, d), bs, off_s)
            shp = (SB_HEADS, SB_DH)
            k, v = proj[:, d:2 * d], proj[:, 2 * d:]
            outs.setdefault("sb", []).append((
                k[:n_p].reshape(bp, seq, *shp), v[:n_p].reshape(bp, seq, *shp),
                k[n_p:].reshape(bs, dec, *shp), v[n_p:].reshape(bs, dec, *shp)))
            w_out = sb_w_out[j]
        x = _outproj(x, jnp.concatenate([o_p, o_s], axis=0), w_out.astype(BF16))
        x = _ffn(x, ffn2_norm[i], ffn2_w_gu[i].astype(BF16), ffn2_w_down[i].astype(BF16))
    stack = lambda key, idx: jnp.stack([t[idx] for t in outs[key]])
    return (x[:n_p].reshape(bp, seq, d), x[n_p:].reshape(bs, dec, d),
            stack("gla", 0), stack("gla", 1),
            stack("band", 0), stack("band", 1), stack("band", 2), stack("band", 3),
            stack("gdn", 0), stack("gdn", 1), stack("gdn", 2), stack("gdn", 3),
            stack("sb", 0), stack("sb", 1), stack("sb", 2), stack("sb", 3))
```

```python
import functools

import jax
import jax.numpy as jnp
import numpy as np
from jax import lax
from jax.experimental import pallas as pl
from jax.experimental.pallas import tpu as pltpu

F32 = jnp.float32
BF16 = jnp.bfloat16

D_MODEL = 1024
CHUNK = 64
EPS = 1e-6
D_FF = 2816
GLA_HEADS, GLA_DK, GLA_DV, GLA_RANK, GLA_TAU = 4, 128, 256, 16, 16.0
BAND_HEADS, BAND_DH, LEFT_CHUNKS, REL_CLIP = 16, 64, 8, 256
BAND_WINDOW = LEFT_CHUNKS * CHUNK
BAND_SPAN = BAND_WINDOW + CHUNK
GDN_HEADS, GDN_DK, GDN_DV, CONV_W = 8, 128, 128, 4
GDN_CONV_CH = 2 * GDN_HEADS * GDN_DK + GDN_HEADS * GDN_DV
SB_HEADS, SB_DH = 16, 64
LOG2E = 1.4426950408889634

LANES = 128
SUBLANES = 8
VMEM_LIMIT = 56 * 1024 * 1024
ROW_TILE = 512
SB_TK = 256


def _params(*sem):
    return pltpu.CompilerParams(dimension_semantics=sem, vmem_limit_bytes=VMEM_LIMIT)


def _dot(a, b):
    return jnp.dot(a, b, preferred_element_type=F32)


def _dot_nt(a, b):
    return lax.dot_general(a, b, (((1,), (1,)), ((), ())), preferred_element_type=F32)


def _dot_tn(a, b):
    return lax.dot_general(a, b, (((0,), (0,)), ((), ())), preferred_element_type=F32)


def _split3(x):
    h1 = x.astype(BF16)
    r1 = x - h1.astype(F32)
    h2 = r1.astype(BF16)
    h3 = (r1 - h2.astype(F32)).astype(BF16)
    return h1, h2, h3


def _tri_dot(tri, x):
    h1, h2, h3 = _split3(x)
    return _dot(tri, h1) + _dot(tri, h2) + _dot(tri, h3)


def _dot_f32(a, b):
    a1 = a.astype(BF16)
    a2 = (a - a1.astype(F32)).astype(BF16)
    b1 = b.astype(BF16)
    b2 = (b - b1.astype(F32)).astype(BF16)
    return _dot(a1, b1) + (_dot(a1, b2) + _dot(a2, b1))


def _sigmoid(x):
    return 1.0 / (1.0 + jnp.exp(-x))


def _silu(x):
    return x * _sigmoid(x)


def _softplus(x):
    return jnp.maximum(x, 0.0) + jnp.log1p(jnp.exp(-jnp.abs(x)))


def _log_sigmoid(x):
    return -_softplus(-x)


def _rms(x, g):
    return x * lax.rsqrt(jnp.mean(x * x, axis=-1, keepdims=True) + EPS) * g


def _iota2(shape, dim):
    return lax.broadcasted_iota(jnp.int32, shape, dim)


def _ffn_kernel(x_ref, g_ref, wg_ref, wu_ref, wd_ref, o_ref):
    x = x_ref[...]
    xn = _rms(x, g_ref[...]).astype(BF16)
    g = _dot(xn, wg_ref[...])
    u = _dot(xn, wu_ref[...])
    h = (_silu(g) * u).astype(BF16)
    o_ref[...] = x + 0.5 * _dot(h, wd_ref[...])


def _ffn(x, gain, w_gu, w_down):
    t, d = x.shape
    once = dict(pipeline_mode=pl.Buffered(1))
    return pl.pallas_call(
        _ffn_kernel,
        out_shape=jax.ShapeDtypeStruct((t, d), F32),
        grid=(t // ROW_TILE,),
        in_specs=[
            pl.BlockSpec((ROW_TILE, d), lambda i: (i, 0)),
            pl.BlockSpec((1, d), lambda i: (0, 0)),
            pl.BlockSpec((d, D_FF), lambda i: (0, 0), **once),
            pl.BlockSpec((d, D_FF), lambda i: (0, 1), **once),
            pl.BlockSpec((D_FF, d), lambda i: (0, 0), **once),
        ],
        out_specs=pl.BlockSpec((ROW_TILE, d), lambda i: (i, 0)),
        compiler_params=_params("parallel"),
        name="ffn",
    )(x, gain.reshape(1, d), w_gu, w_gu, w_down)


def _pair_head_norm(x, gain2):
    low = _iota2(x.shape, 1) < BAND_DH
    sq = x * x
    s_lo = jnp.sum(jnp.where(low, sq, 0.0), axis=-1, keepdims=True)
    s_hi = jnp.sum(jnp.where(low, 0.0, sq), axis=-1, keepdims=True)
    ms = jnp.where(low, s_lo, s_hi) * (1.0 / BAND_DH)
    return x * lax.rsqrt(ms + EPS) * gain2


def _proj_kernel(*refs, segs, n_gain):
    x_ref, g_ref, w_ref = refs[:3]
    gains = refs[3:3 + n_gain]
    outs = refs[3 + n_gain:]
    xn = _rms(x_ref[...], g_ref[...]).astype(BF16)
    for c0, c1, gain_idx, dsts in segs:
        y = _dot(xn, w_ref[:, c0:c1])
        if gain_idx is not None:
            y = jnp.concatenate([_pair_head_norm(y[:, s:s + LANES], gains[gain_idx][...])
                                 for s in range(0, c1 - c0, LANES)], axis=1)
        for out_idx, off, scale in dsts:
            o_ref = outs[out_idx]
            o_ref[:, off:off + c1 - c0] = (y if scale == 1.0 else y * scale).astype(o_ref.dtype)


def _proj(x, gain, w, out_defs, segs, gains=()):
    t, d = x.shape
    row = lambda width: pl.BlockSpec((ROW_TILE, width), lambda i: (i, 0))
    const = lambda shape: pl.BlockSpec(shape, lambda i: (0, 0))
    return pl.pallas_call(
        functools.partial(_proj_kernel, segs=tuple(segs), n_gain=len(gains)),
        out_shape=tuple(jax.ShapeDtypeStruct((t, width), dt) for width, dt in out_defs),
        grid=(t // ROW_TILE,),
        in_specs=[row(d), const((1, d)), pl.BlockSpec(w.shape, lambda i: (0, 0), pipeline_mode=pl.Buffered(1))]
        + [const((1, LANES))] * len(gains),
        out_specs=tuple(row(width) for width, _ in out_defs),
        compiler_params=_params("parallel"),
        name="proj",
    )(x, gain.reshape(1, d), w, *[jnp.tile(g, 2).reshape(1, LANES) for g in gains])


def _col_segs(n, dsts_of):
    return [(c, min(c + D_MODEL, n), None, dsts_of(c)) for c in range(0, n, D_MODEL)]


def _outproj_kernel(res_ref, a_ref, w_ref, o_ref):
    o_ref[...] = res_ref[...] + _dot(a_ref[...], w_ref[...])


def _outproj(res, a, w):
    t, d = res.shape
    return pl.pallas_call(
        _outproj_kernel,
        out_shape=jax.ShapeDtypeStruct((t, d), F32),
        grid=(t // ROW_TILE,),
        in_specs=[
            pl.BlockSpec((ROW_TILE, d), lambda i: (i, 0)),
            pl.BlockSpec((ROW_TILE, a.shape[1]), lambda i: (i, 0)),
            pl.BlockSpec(w.shape, lambda i: (0, 0)),
        ],
        out_specs=pl.BlockSpec((ROW_TILE, d), lambda i: (i, 0)),
        compiler_params=_params("parallel"),
        name="outproj",
    )(res, a, w)


def _gla_kernel(*refs, has_init):
    if has_init:
        (q_ref, k_ref, v_ref, r_ref, gl_ref, wg2_ref, bg_ref, on_ref, s0_ref,
         o_ref, sfin_ref, st_ref) = refs
    else:
        (q_ref, k_ref, v_ref, r_ref, gl_ref, wg2_ref, bg_ref, on_ref,
         o_ref, sfin_ref, st_ref) = refs
    c = pl.program_id(1)

    @pl.when(c == 0)
    def _():
        for h in range(GLA_HEADS):
            if has_init:
                st_ref[h] = s0_ref[0, h].T
            else:
                st_ref[h] = jnp.zeros((GLA_DV, GLA_DK), F32)

    incl = _iota2((CHUNK, CHUNK), 0) >= _iota2((CHUNK, CHUNK), 1)
    tri = jnp.where(incl, 1.0, 0.0).astype(BF16)
    x = _dot(gl_ref[...].astype(BF16), wg2_ref[...]) + bg_ref[...]
    log_a = _log_sigmoid(x) * (1.0 / GLA_TAU)
    b = _tri_dot(tri, log_a)
    b_last = b[CHUNK - 1:CHUNK, :]
    q = q_ref[...] * (GLA_DK ** -0.5)
    k = k_ref[...]
    q_e = (q * jnp.exp(b)).astype(BF16)
    k_e = (k * jnp.exp(-b)).astype(BF16)
    k_l = (k * jnp.exp(b_last - b)).astype(BF16)
    d_last = jnp.exp(b_last)
    gain = on_ref[...]
    for h in range(GLA_HEADS):
        ks = slice(h * GLA_DK, (h + 1) * GLA_DK)
        vs = slice(h * GLA_DV, (h + 1) * GLA_DV)
        v = v_ref[:, vs].astype(BF16)
        st = st_ref[h]
        att = jnp.where(incl, _dot_nt(q_e[:, ks], k_e[:, ks]), 0.0)
        o = _dot(att.astype(BF16), v) + _dot_nt(q_e[:, ks], st.astype(BF16))
        st_ref[h] = st * d_last[:, ks] + _dot_tn(v, k_l[:, ks])
        o = _rms(o, gain) * _silu(r_ref[:, vs])
        o_ref[:, vs] = o.astype(BF16)

    @pl.when(c == pl.num_programs(1) - 1)
    def _():
        for h in range(GLA_HEADS):
            sfin_ref[0, h] = st_ref[h].T


def _gla(proj, w_gate2, b_gate, onorm, s0, nb, nc, row_off):
    qk = GLA_HEADS * GLA_DK
    vv = GLA_HEADS * GLA_DV
    row = lambda b, c: row_off + b * nc + c
    in_specs = [
        pl.BlockSpec((CHUNK, qk), lambda b, c: (row(b, c), 0)),
        pl.BlockSpec((CHUNK, qk), lambda b, c: (row(b, c), 1)),
        pl.BlockSpec((CHUNK, vv), lambda b, c: (row(b, c), 1)),
        pl.BlockSpec((CHUNK, vv), lambda b, c: (row(b, c), 2)),
        pl.BlockSpec((CHUNK, LANES), lambda b, c: (row(b, c), (2 * qk + 2 * vv) // LANES)),
        pl.BlockSpec((LANES, qk), lambda b, c: (0, 0)),
        pl.BlockSpec((1, qk), lambda b, c: (0, 0)),
        pl.BlockSpec((1, GLA_DV), lambda b, c: (0, 0)),
    ]
    args = [proj, proj, proj, proj, proj, w_gate2, b_gate, onorm]
    if s0 is not None:
        in_specs.append(pl.BlockSpec((1, GLA_HEADS, GLA_DK, GLA_DV), lambda b, c: (b, 0, 0, 0)))
        args.append(s0)
    return pl.pallas_call(
        functools.partial(_gla_kernel, has_init=s0 is not None),
        out_shape=(jax.ShapeDtypeStruct((nb * nc * CHUNK, vv), BF16),
                   jax.ShapeDtypeStruct((nb, GLA_HEADS, GLA_DK, GLA_DV), F32)),
        grid=(nb, nc),
        in_specs=in_specs,
        out_specs=(pl.BlockSpec((CHUNK, vv), lambda b, c: (b * nc + c, 0)),
                   pl.BlockSpec((1, GLA_HEADS, GLA_DK, GLA_DV), lambda b, c: (b, 0, 0, 0))),
        scratch_shapes=[pltpu.VMEM((GLA_HEADS, GLA_DV, GLA_DK), F32)],
        compiler_params=_params("parallel", "arbitrary"),
        name="gla",
    )(*args)


BIAS_WIDTH = 640


def _band_bias_kernel(w_ref, o_ref):
    for h in range(BAND_HEADS):
        x = jnp.broadcast_to(w_ref[h:h + 1, :], (CHUNK, BIAS_WIDTH))
        rolled = pltpu.roll(x, 0, 1, stride=1, stride_axis=0)
        o_ref[h // 2, (h % 2) * CHUNK:(h % 2 + 1) * CHUNK, :] = rolled[:, 0:BAND_SPAN]


def _band_bias(rel_bias):
    c = jnp.arange(BIAS_WIDTH)
    n = jnp.where(c <= BAND_SPAN, c, c - BIAS_WIDTH)
    w = rel_bias[:, jnp.clip(BAND_WINDOW - n, -REL_CLIP, REL_CLIP) + REL_CLIP]
    return pl.pallas_call(
        _band_bias_kernel,
        out_shape=jax.ShapeDtypeStruct((BAND_HEADS // 2, 2 * CHUNK, BAND_SPAN), F32),
        name="band_bias",
    )(w)


def _band_attn_kernel(q_ref, kp_ref, kc_ref, vp_ref, vc_ref, bias_ref, o_ref, kall_ref, vall_ref,
                      *, nq, prompt):
    if prompt:
        kall_ref[0:BAND_WINDOW] = kp_ref[...]
        vall_ref[0:BAND_WINDOW] = vp_ref[...]
    else:
        kall_ref[0:BAND_WINDOW] = kp_ref[0].astype(BF16)
        vall_ref[0:BAND_WINDOW] = vp_ref[0].astype(BF16)
    kall_ref[BAND_WINDOW:] = kc_ref[...]
    vall_ref[BAND_WINDOW:] = vc_ref[...]
    first_pos = (pl.program_id(1) - 1) * BAND_WINDOW if prompt else 0
    key_idx = _iota2((2 * CHUNK, BAND_SPAN), 1)
    pairs = range(D_MODEL // LANES)
    lanes = [slice(p * LANES, (p + 1) * LANES) for p in pairs]

    def chunk(t, carry):
        r0 = pl.multiple_of(t * CHUNK, CHUNK)
        qs = [_stack_pair(q_ref[pl.ds(r0, CHUNK), ls]) for ls in lanes]
        s = [_dot_nt(qs[p], kall_ref[pl.ds(r0, BAND_SPAN), lanes[p]]) * (BAND_DH ** -0.5) + bias_ref[p]
             for p in pairs]
        if prompt:
            visible = first_pos + r0 + key_idx >= 0
            s = [jnp.where(visible, x, -jnp.inf) for x in s]
        e = [jnp.exp(x - jnp.max(x, axis=-1, keepdims=True)) for x in s]
        prob = [(x / jnp.sum(x, axis=-1, keepdims=True)).astype(BF16) for x in e]
        pv = [_dot(prob[p], vall_ref[pl.ds(r0, BAND_SPAN), lanes[p]]) for p in pairs]
        for p in pairs:
            o_ref[pl.ds(r0, CHUNK), lanes[p]] = _unstack_pair(pv[p]).astype(BF16)
        return carry

    lax.fori_loop(0, nq, chunk, 0)


def _band_attn_prompt(qn, knb, vb, bias, nb, seq):
    d = D_MODEL
    blk = BAND_WINDOW
    nblk = seq // blk
    cur = pl.BlockSpec((blk, d), lambda b, i: (b * nblk + i, 0))
    prev = pl.BlockSpec((blk, d), lambda b, i: (b * nblk + jnp.maximum(i - 1, 0), 0))
    return pl.pallas_call(
        functools.partial(_band_attn_kernel, nq=blk // CHUNK, prompt=True),
        out_shape=jax.ShapeDtypeStruct((nb * seq, d), BF16),
        grid=(nb, nblk),
        in_specs=[cur, prev, cur, prev, cur,
                  pl.BlockSpec(bias.shape, lambda b, i: (0, 0, 0))],
        out_specs=cur,
        scratch_shapes=[pltpu.VMEM((2 * blk, d), BF16), pltpu.VMEM((2 * blk, d), BF16)],
        compiler_params=_params("parallel", "arbitrary"),
        name="band_attn_prompt",
    )(qn, knb, knb, vb, vb, bias)


def _band_attn_sample(qn, knb, vb, cache_k, cache_v, bias, nb, row_off):
    d = D_MODEL
    new = pl.BlockSpec((CHUNK, d), lambda b: (row_off + b, 0))
    old = pl.BlockSpec((1, BAND_WINDOW, d), lambda b: (b, 0, 0))
    return pl.pallas_call(
        functools.partial(_band_attn_kernel, nq=1, prompt=False),
        out_shape=jax.ShapeDtypeStruct((nb * CHUNK, d), BF16),
        grid=(nb,),
        in_specs=[new, old, new, old, new, pl.BlockSpec(bias.shape, lambda b: (0, 0, 0))],
        out_specs=pl.BlockSpec((CHUNK, d), lambda b: (b, 0)),
        scratch_shapes=[pltpu.VMEM((BAND_SPAN, d), BF16), pltpu.VMEM((BAND_SPAN, d), BF16)],
        compiler_params=_params("parallel"),
        name="band_attn_sample",
    )(qn, cache_k, knb, cache_v, vb, bias)


def _gdn_prep_kernel(*refs, has_init):
    if has_init:
        x_ref, raw_ref, cw_ref, par_ref, c0_ref, q_ref, k_ref, v_ref, bg_ref, carry_ref = refs
    else:
        x_ref, raw_ref, cw_ref, par_ref, q_ref, k_ref, v_ref, bg_ref, carry_ref = refs
    tb = x_ref.shape[0]

    @pl.when(pl.program_id(1) == 0)
    def _():
        if has_init:
            carry_ref[...] = c0_ref[0]
        else:
            carry_ref[...] = jnp.zeros_like(carry_ref)

    x = x_ref[...]
    w = cw_ref[...]
    x8 = x[0:SUBLANES]
    c8 = carry_ref[...]
    row8 = _iota2(x8.shape, 0)
    y = x * w[CONV_W - 1:CONV_W]
    y8 = x8 * w[CONV_W - 1:CONV_W]
    for s in range(1, CONV_W):
        ws = w[CONV_W - 1 - s:CONV_W - s]
        y = y + pltpu.roll(x, s, 0) * ws
        y8 = y8 + jnp.where(row8 < s, pltpu.roll(c8, s, 0), pltpu.roll(x8, s, 0)) * ws
    carry_ref[...] = x[tb - SUBLANES:tb]
    if tb > SUBLANES:
        y = jnp.concatenate([y8, y[SUBLANES:]], axis=0)
    else:
        y = y8
    y = _silu(y)
    qk = GDN_HEADS * GDN_DK
    for h in range(GDN_HEADS):
        qh = y[:, h * GDN_DK:(h + 1) * GDN_DK]
        kh = y[:, qk + h * GDN_DK:qk + (h + 1) * GDN_DK]
        q_ref[h] = qh * lax.rsqrt(jnp.sum(qh * qh, axis=-1, keepdims=True) + EPS)
        k_ref[h] = kh * lax.rsqrt(jnp.sum(kh * kh, axis=-1, keepdims=True) + EPS)
        v_ref[h] = y[:, 2 * qk + h * GDN_DV:2 * qk + (h + 1) * GDN_DV]
    raw = raw_ref[...]
    beta = _sigmoid(raw)
    g = par_ref[0:1, :] * _softplus(raw + par_ref[1:2, :])
    bg_ref[...] = jnp.where(_iota2(raw.shape, 1) < GDN_HEADS, beta, g)


def _gdn_prep(proj, conv_w, par, c0, nb, seq, row_off, tb):
    nblk = seq // tb
    base = row_off // tb
    row = lambda b, i: base + b * nblk + i
    in_specs = [
        pl.BlockSpec((tb, GDN_CONV_CH), lambda b, i: (row(b, i), 0)),
        pl.BlockSpec((tb, LANES), lambda b, i: (row(b, i), (GDN_CONV_CH + GDN_HEADS * GDN_DV) // LANES)),
        pl.BlockSpec((CONV_W, GDN_CONV_CH), lambda b, i: (0, 0)),
        pl.BlockSpec((2, LANES), lambda b, i: (0, 0)),
    ]
    args = [proj, proj, conv_w, par]
    if c0 is not None:
        in_specs.append(pl.BlockSpec((1, SUBLANES, GDN_CONV_CH), lambda b, i: (b, 0, 0)))
        args.append(c0)
    n = nb * seq
    head = pl.BlockSpec((GDN_HEADS, tb, GDN_DK), lambda b, i: (0, b * nblk + i, 0))
    return pl.pallas_call(
        functools.partial(_gdn_prep_kernel, has_init=c0 is not None),
        out_shape=(jax.ShapeDtypeStruct((GDN_HEADS, n, GDN_DK), F32),
                   jax.ShapeDtypeStruct((GDN_HEADS, n, GDN_DK), F32),
                   jax.ShapeDtypeStruct((GDN_HEADS, n, GDN_DV), F32),
                   jax.ShapeDtypeStruct((n, LANES), F32)),
        grid=(nb, nblk),
        in_specs=in_specs,
        out_specs=(head, head, head, pl.BlockSpec((tb, LANES), lambda b, i: (b * nblk + i, 0))),
        scratch_shapes=[pltpu.VMEM((SUBLANES, GDN_CONV_CH), F32)],
        compiler_params=_params("parallel", "arbitrary"),
        name="gdn_prep",
    )(*args)


def _gdn_kernel(*refs, has_init):
    if has_init:
        q_ref, k_ref, v_ref, bg_ref, gate_ref, on_ref, s0_ref, o_ref, sfin_ref, s_ref = refs
    else:
        q_ref, k_ref, v_ref, bg_ref, gate_ref, on_ref, o_ref, sfin_ref, s_ref = refs
    c = pl.program_id(1)

    @pl.when(c == 0)
    def _():
        if has_init:
            s_ref[...] = s0_ref[0]
        else:
            s_ref[...] = jnp.zeros_like(s_ref)

    ri = _iota2((CHUNK, CHUNK), 0)
    ci = _iota2((CHUNK, CHUNK), 1)
    incl = ri >= ci
    strict = ri > ci
    tri = jnp.where(incl, 1.0, 0.0).astype(BF16)
    eye = jnp.where(ri == ci, 1.0, 0.0)
    bg = bg_ref[...]
    cs = _tri_dot(tri, bg)
    cs_t = cs.T
    gain = on_ref[...]
    heads = range(GDN_HEADS)
    beta = [bg[:, h:h + 1] for h in heads]
    gcol = [cs[:, GDN_HEADS + h:GDN_HEADS + h + 1] for h in heads]
    glast = [g[CHUNK - 1:CHUNK, :] for g in gcol]
    gam = [jnp.exp(jnp.where(incl, gcol[h] - cs_t[GDN_HEADS + h:GDN_HEADS + h + 1, :], -jnp.inf)) for h in heads]
    egc = [jnp.exp(g) for g in gcol]
    k = [k_ref[h] for h in heads]
    k16 = [x.astype(BF16) for x in k]
    kb = [k[h] * beta[h] for h in heads]
    m = [jnp.where(strict, _dot_nt(kb[h].astype(BF16), k16[h]) * gam[h], 0.0) for h in heads]
    d = [eye] * GDN_HEADS
    for lvl in range(6):
        sub = (((ri >> lvl) & 1) == 1) & (((ci >> lvl) & 1) == 0) & ((ri >> (lvl + 1)) == (ci >> (lvl + 1)))
        y = [_dot_f32(d[h], jnp.where(sub, m[h], 0.0)) for h in heads]
        x = [_dot_f32(y[h], d[h]) for h in heads]
        d = [d[h] - x[h] for h in heads]
    t16 = [t.astype(BF16) for t in d]
    u = [_dot(t16[h], (v_ref[h] * beta[h]).astype(BF16)) for h in heads]
    w = [_dot(t16[h], (kb[h] * egc[h]).astype(BF16)) for h in heads]
    q = [q_ref[h] * (GDN_DK ** -0.5) for h in heads]
    att = [_dot_nt(q[h].astype(BF16), k16[h]) * gam[h] for h in heads]
    q_e = [(q[h] * egc[h]).astype(BF16) for h in heads]
    k_l = [(k[h] * jnp.exp(glast[h] - gcol[h])).astype(BF16) for h in heads]
    s = [s_ref[h] for h in heads]
    s16 = [x.astype(BF16) for x in s]
    v_new = [u[h] - _dot(w[h].astype(BF16), s16[h]) for h in heads]
    v16 = [x.astype(BF16) for x in v_new]
    o = [_dot(q_e[h], s16[h]) + _dot(att[h].astype(BF16), v16[h]) for h in heads]
    for h in heads:
        s_ref[h] = s[h] * jnp.exp(glast[h]) + _dot_tn(k_l[h], v16[h])
        vs = slice(h * GDN_DV, (h + 1) * GDN_DV)
        o_ref[:, vs] = (_rms(o[h], gain) * _silu(gate_ref[:, vs])).astype(BF16)

    @pl.when(c == pl.num_programs(1) - 1)
    def _():
        sfin_ref[0] = s_ref[...]


def _gdn(qh, kh, vh, bg, proj, onorm, s0, nb, nc, row_off):
    vv = GDN_HEADS * GDN_DV
    head = pl.BlockSpec((GDN_HEADS, CHUNK, GDN_DK), lambda b, c: (0, b * nc + c, 0))
    in_specs = [
        head, head, head,
        pl.BlockSpec((CHUNK, LANES), lambda b, c: (b * nc + c, 0)),
        pl.BlockSpec((CHUNK, vv), lambda b, c: (row_off + b * nc + c, GDN_CONV_CH // vv)),
        pl.BlockSpec((1, GDN_DV), lambda b, c: (0, 0)),
    ]
    args = [qh, kh, vh, bg, proj, onorm]
    state = pl.BlockSpec((1, GDN_HEADS, GDN_DK, GDN_DV), lambda b, c: (b, 0, 0, 0))
    if s0 is not None:
        in_specs.append(state)
        args.append(s0)
    return pl.pallas_call(
        functools.partial(_gdn_kernel, has_init=s0 is not None),
        out_shape=(jax.ShapeDtypeStruct((nb * nc * CHUNK, vv), BF16),
                   jax.ShapeDtypeStruct((nb, GDN_HEADS, GDN_DK, GDN_DV), F32)),
        grid=(nb, nc),
        in_specs=in_specs,
        out_specs=(pl.BlockSpec((CHUNK, vv), lambda b, c: (b * nc + c, 0)), state),
        scratch_shapes=[pltpu.VMEM((GDN_HEADS, GDN_DK, GDN_DV), F32)],
        compiler_params=_params("parallel", "arbitrary"),
        name="gdn",
    )(*args)


def _neg_abs(x):
    bits = lax.bitcast_convert_type(x, jnp.uint32) | jnp.uint32(0x80000000)
    return lax.bitcast_convert_type(bits, F32)


def _stack_pair(q2):
    low = _iota2(q2.shape, 1) < SB_DH
    return jnp.concatenate([jnp.where(low, q2, 0), jnp.where(low, 0, q2)], axis=0)


def _unstack_pair(acc):
    tq = acc.shape[0] // 2
    low = _iota2((tq, LANES), 1) < SB_DH
    return jnp.where(low, acc[:tq], acc[tq:])


def _upper(tk):
    return jnp.where(_iota2((tk, tk), 0) > _iota2((tk, tk), 1), 1.0, 0.0).astype(BF16)


def _diag_mask(tq, tk):
    r = _iota2((2 * tq, tk), 0)
    r = jnp.where(r >= tq, r - tq, r)
    return _iota2((2 * tq, tk), 1) < r


SB_NEUTRAL = 1e30
SB_LAG = 4


def _sb_log_terms(zn, mask):
    l1m = jnp.minimum(zn, 0.0) - jnp.log2(1.0 + jnp.exp2(_neg_abs(zn)))
    log_b = l1m - zn
    if mask is not None:
        l1m = jnp.where(mask, l1m, 0.0)
        log_b = jnp.where(mask, log_b, -SB_NEUTRAL)
    return l1m, log_b


def _sb_direct(qs, k2, v2, mask):
    l1m, log_b = _sb_log_terms(_dot_nt(qs, k2), mask)
    a = jnp.exp2(log_b + _dot(l1m.astype(BF16), _upper(k2.shape[0])))
    return jnp.sum(l1m, axis=-1, keepdims=True), _dot(a.astype(BF16), v2)


def _sb_pipeline(nhalf, nchain, q_of, k_of, v_of, st_of, upper, scratch, mask):
    z_ref, hi_ref, logb_ref, tot_ref, cs_ref, a_ref, run_ref, acc_ref = scratch
    logb_ref[...] = jnp.full(logb_ref.shape, -SB_NEUTRAL, F32)
    tot_ref[...] = jnp.zeros(tot_ref.shape, F32)
    for c in range(nchain):
        z_ref[2 * c + 1] = jnp.full(z_ref.shape[1:], SB_NEUTRAL, F32)
        hi_ref[2 * c + 1] = jnp.zeros(hi_ref.shape[1:], BF16)
        cs_ref[2 * c + 1] = jnp.zeros(cs_ref.shape[1:], F32)
        a_ref[2 * c + 1] = jnp.zeros(a_ref.shape[1:], BF16)

    def half(h, p):
        for c in range(nchain):
            rows = st_of(h - 4, c)
            acc_ref[rows, :] += _dot(a_ref[2 * c + 1 - p], v_of(h - 4, c))
        for c in range(nchain):
            rows = st_of(h - 3, c)
            run = run_ref[rows, :]
            a_ref[2 * c + p] = jnp.exp2(cs_ref[2 * c + 1 - p] + (logb_ref[2 * c + p] + run)).astype(BF16)
            run_ref[rows, :] = run + tot_ref[2 * c + p]
        for c in range(nchain):
            cs_ref[2 * c + p] = _dot(hi_ref[2 * c + 1 - p], upper)
        for c in range(nchain):
            l1m, logb_ref[2 * c + p] = _sb_log_terms(z_ref[2 * c + 1 - p], mask)
            hi_ref[2 * c + p] = l1m.astype(BF16)
            tot_ref[2 * c + p] = jnp.sum(l1m, axis=-1, keepdims=True)
        for c in range(nchain):
            z_ref[2 * c + p] = _dot_nt(q_of(h, c), k_of(h, c))

    def body(i, _):
        half(2 * i, 0)
        half(2 * i + 1, 1)
        return 0

    assert nhalf % 2 == 0
    lax.fori_loop(0, nhalf // 2, body, 0)


def _sb_rings(nchain, rows, tk):
    slots = 2 * nchain
    return [pltpu.VMEM((slots, rows, tk), F32), pltpu.VMEM((slots, rows, tk), BF16),
            pltpu.VMEM((slots, rows, tk), F32), pltpu.VMEM((slots, rows, 1), F32),
            pltpu.VMEM((slots, rows, tk), F32), pltpu.VMEM((slots, rows, tk), BF16)]


def _sb_prompt_items(nq):
    diag = [(i, i) for i in range(nq)]
    off = [(j, i) for j in range(nq - 2, -1, -1) for i in range(j + 1, nq)]
    pad = [(0, nq)] * (2 * SB_LAG)
    assert len(diag) % 4 == 0 and len(off) % 4 == 0
    rows = []
    for items in (diag, off):
        padded = pad + items + pad
        rows += [[j for j, _ in padded], [i for _, i in padded]]
    width = max(len(r) for r in rows)
    tab = np.zeros((4, width), np.int32)
    for r, row in enumerate(rows):
        tab[r, :len(row)] = row
    return jnp.asarray(tab), len(diag), len(off)


def _sb_prompt_kernel(tab_ref, q_ref, k_ref, v_ref, o_ref, qs_ref, run_ref, acc_ref, *rings, nq, ndiag, noff):
    tk = SB_TK
    rows = 2 * tk

    def stack(i, _):
        qs_ref[pl.ds(pl.multiple_of(i * rows, rows), rows), :] = _stack_pair(
            q_ref[pl.ds(pl.multiple_of(i * tk, tk), tk), :])
        return 0

    lax.fori_loop(0, nq, stack, 0)
    qs_ref[nq * rows:(nq + 1) * rows, :] = jnp.zeros((rows, LANES), BF16)
    run_ref[...] = jnp.zeros(run_ref.shape, F32)
    acc_ref[...] = jnp.zeros(acc_ref.shape, F32)
    upper = _upper(tk)
    scratch = list(rings) + [run_ref, acc_ref]
    for row, nitems, mask in ((0, ndiag, _diag_mask(tk, tk)), (2, noff, None)):
        col = lambda hh, c: 2 * hh + c + 2 * SB_LAG
        q_rows = lambda hh, c: pl.ds(pl.multiple_of(tab_ref[row + 1, col(hh, c)] * rows, rows), rows)
        k_rows = lambda hh, c: pl.ds(pl.multiple_of(tab_ref[row, col(hh, c)] * tk, tk), tk)
        _sb_pipeline(nitems // 2 + SB_LAG, 2,
                     lambda hh, c: qs_ref[q_rows(hh, c), :],
                     lambda hh, c: k_ref[k_rows(hh, c), :],
                     lambda hh, c: v_ref[k_rows(hh, c), :],
                     q_rows, upper, scratch, mask)

    def emit(i, _):
        o_ref[pl.ds(pl.multiple_of(i * tk, tk), tk), :] = _unstack_pair(
            acc_ref[pl.ds(pl.multiple_of(i * rows, rows), rows), :]).astype(BF16)
        return 0

    lax.fori_loop(0, nq, emit, 0)


def _sb_prompt(qb, kb, vb, nb, seq):
    d = D_MODEL
    npair = d // LANES
    nq = seq // SB_TK
    rows = 2 * SB_TK
    tab, ndiag, noff = _sb_prompt_items(nq)
    col = pl.BlockSpec((seq, LANES), lambda b, p, tab: (b, p))
    return pl.pallas_call(
        functools.partial(_sb_prompt_kernel, nq=nq, ndiag=ndiag, noff=noff),
        out_shape=jax.ShapeDtypeStruct((nb * seq, d), BF16),
        grid_spec=pltpu.PrefetchScalarGridSpec(
            num_scalar_prefetch=1,
            grid=(nb, npair),
            in_specs=[col, col, col],
            out_specs=col,
            scratch_shapes=[pltpu.VMEM(((nq + 1) * rows, LANES), BF16), pltpu.VMEM(((nq + 1) * rows, 1), F32),
                            pltpu.VMEM(((nq + 1) * rows, LANES), F32)] + _sb_rings(2, rows, SB_TK)),
        compiler_params=_params("parallel", "arbitrary"),
        name="sb_prompt",
    )(tab, qb, kb, vb)


def _sb_sample_kernel(q_ref, kn_ref, vn_ref, kc_ref, vc_ref, o_ref, run_ref, acc_ref, *rings):
    tq, tk = CHUNK, SB_TK
    rows = 2 * tq
    nblk = kc_ref.shape[1] // tk
    lanes = lambda c: slice(c * LANES, (c + 1) * LANES)
    qs = [_stack_pair(q_ref[:, lanes(c)]) for c in range(2)]
    for c in range(2):
        run, acc = _sb_direct(qs[c], kn_ref[:, lanes(c)], vn_ref[:, lanes(c)], _diag_mask(tq, tq))
        run_ref[c * rows:(c + 1) * rows, :] = run
        acc_ref[c * rows:(c + 1) * rows, :] = acc

    def k_rows(hh):
        return pl.ds(pl.multiple_of(jnp.clip(nblk - 1 - hh, 0, nblk - 1) * tk, tk), tk)

    def st_of(hh, c):
        real = (hh >= 0) & (hh < nblk)
        return pl.ds(pl.multiple_of(jnp.where(real, c, 2) * rows, rows), rows)

    _sb_pipeline(nblk + SB_LAG, 2,
                 lambda hh, c: qs[c],
                 lambda hh, c: kc_ref[0, k_rows(hh), lanes(c)].astype(BF16),
                 lambda hh, c: vc_ref[0, k_rows(hh), lanes(c)].astype(BF16),
                 st_of, _upper(tk), list(rings) + [run_ref, acc_ref], None)
    for c in range(2):
        o_ref[:, lanes(c)] = _unstack_pair(acc_ref[c * rows:(c + 1) * rows, :]).astype(BF16)


def _sb_sample(qb, kb, vb, cache_k, cache_v, nb, row_off):
    d = D_MODEL
    wide = 2 * LANES
    ngrp = d // wide
    past = cache_k.shape[1]
    rows = 2 * CHUNK
    new = pl.BlockSpec((CHUNK, wide), lambda b, g: (row_off + b, g))
    old = pl.BlockSpec((1, past, wide), lambda b, g: (b, 0, g))
    return pl.pallas_call(
        _sb_sample_kernel,
        out_shape=jax.ShapeDtypeStruct((nb * CHUNK, d), BF16),
        grid=(nb, ngrp),
        in_specs=[new, new, new, old, old],
        out_specs=pl.BlockSpec((CHUNK, wide), lambda b, g: (b, g)),
        scratch_shapes=[pltpu.VMEM((3 * rows, 1), F32), pltpu.VMEM((3 * rows, LANES), F32)]
        + _sb_rings(2, rows, SB_TK),
        compiler_params=_params("parallel", "parallel"),
        name="sb_sample",
    )(qb, kb, vb, cache_k, cache_v)


def _pad_cols(w, n):
    return jnp.pad(w, ((0, 0), (0, n - w.shape[1])))


def kernel(x_prompt, x_sample, state_gla, cache_band_k, cache_band_v, state_gdn, state_gdn_conv,
           cache_sb_k, cache_sb_v, ffn1_norm, ffn1_w_gu, ffn1_w_down, mix_norm, ffn2_norm,
           ffn2_w_gu, ffn2_w_down, gla_w_in, gla_w_gate2, gla_b_gate, gla_onorm, gla_w_out,
           band_w_in, band_q_norm, band_k_norm, band_rel_bias, band_w_out, gdn_w_in, gdn_conv_w,
           gdn_a_log, gdn_dt_bias, gdn_onorm, gdn_w_out, sb_w_in, sb_w_out):
    bp, seq, d = x_prompt.shape
    bs, dec = x_sample.shape[:2]
    assert dec == CHUNK and seq % BAND_WINDOW == 0 and d == D_MODEL
    n_p = bp * seq
    n_s = bs * dec
    nc_p = seq // CHUNK
    off_s = n_p // CHUNK
    x = jnp.concatenate([x_prompt.reshape(n_p, d), x_sample.reshape(n_s, d)], axis=0)
    depth = ffn1_norm.shape[0]
    outs = {}
    for i in range(depth):
        m, j = i % 4, i // 4
        x = _ffn(x, ffn1_norm[i], ffn1_w_gu[i].astype(BF16), ffn1_w_down[i].astype(BF16))
        if m == 0:
            n_in = 2 * GLA_HEADS * GLA_DK + 2 * GLA_HEADS * GLA_DV + LANES
            (proj,) = _proj(x, mix_norm[i], _pad_cols(gla_w_in[j], n_in).astype(BF16), [(n_in, F32)],
                            _col_segs(n_in, lambda c: [(0, c, 1.0)]))
            wg2 = jnp.pad(gla_w_gate2[j], ((0, LANES - GLA_RANK), (0, 0))).astype(BF16)
            common = (wg2, gla_b_gate[j].reshape(1, -1), gla_onorm[j].reshape(1, -1))
            o_p, s_p = _gla(proj, *common, None, bp, nc_p, 0)
            o_s, s_s = _gla(proj, *common, state_gla[j], bs, 1, off_s)
            outs.setdefault("gla", []).append((s_p, s_s))
            w_out = gla_w_out[j]
        elif m == 1:
            qn, kn, knb, v, vb = _proj(
                x, mix_norm[i], band_w_in[j].astype(BF16), [(d, BF16), (d, F32), (d, BF16), (d, F32), (d, BF16)],
                [(0, d, 0, [(0, 0, 1.0)]), (d, 2 * d, 1, [(1, 0, 1.0), (2, 0, 1.0)]),
                 (2 * d, 3 * d, None, [(3, 0, 1.0), (4, 0, 1.0)])],
                gains=(band_q_norm[j], band_k_norm[j]))
            bias = _band_bias(band_rel_bias[j])
            o_p = _band_attn_prompt(qn, knb, vb, bias, bp, seq)
            o_s = _band_attn_sample(qn, knb, vb, cache_band_k[j].reshape(bs, BAND_WINDOW, d),
                                    cache_band_v[j].reshape(bs, BAND_WINDOW, d), bias, bs, off_s)
            keep = min(BAND_WINDOW, seq)
            shp = (BAND_HEADS, BAND_DH)
            outs.setdefault("band", []).append((
                kn[:n_p].reshape(bp, seq, *shp)[:, seq - keep:], v[:n_p].reshape(bp, seq, *shp)[:, seq - keep:],
                kn[n_p:].reshape(bs, dec, *shp), v[n_p:].reshape(bs, dec, *shp)))
            w_out = band_w_out[j]
        elif m == 2:
            n_in = GDN_CONV_CH + GDN_HEADS * GDN_DV + LANES
            (proj,) = _proj(x, mix_norm[i], _pad_cols(gdn_w_in[j], n_in).astype(BF16), [(n_in, F32)],
                            _col_segs(n_in, lambda c: [(0, c, 1.0)]))
            par = jnp.zeros((2, LANES), F32)
            par = par.at[0, GDN_HEADS:2 * GDN_HEADS].set(-jnp.exp(gdn_a_log[j]))
            par = par.at[1, GDN_HEADS:2 * GDN_HEADS].set(gdn_dt_bias[j])
            c0 = jnp.pad(state_gdn_conv[j], ((0, 0), (SUBLANES - (CONV_W - 1), 0), (0, 0)))
            pre_p = _gdn_prep(proj, gdn_conv_w[j], par, None, bp, seq, 0, ROW_TILE)
            pre_s = _gdn_prep(proj, gdn_conv_w[j], par, c0, bs, dec, n_p, dec)
            onorm = gdn_onorm[j].reshape(1, -1)
            o_p, s_p = _gdn(*pre_p, proj, onorm, None, bp, nc_p, 0)
            o_s, s_s = _gdn(*pre_s, proj, onorm, state_gdn[j], bs, 1, off_s)
            qkv = proj[:, :GDN_CONV_CH]
            conv_p = qkv[:n_p].reshape(bp, seq, -1)[:, seq - (CONV_W - 1):]
            conv_s = qkv[n_p:].reshape(bs, dec, -1)[:, dec - (CONV_W - 1):]
            outs.setdefault("gdn", []).append((s_p, conv_p, s_s, conv_s))
            w_out = gdn_w_out[j]
        else:
            qb, k, kb, v, vb = _proj(
                x, mix_norm[i], sb_w_in[j].astype(BF16), [(d, BF16), (d, F32), (d, BF16), (d, F32), (d, BF16)],
                [(0, d, None, [(0, 0, -LOG2E * SB_DH ** -0.5)]), (d, 2 * d, None, [(1, 0, 1.0), (2, 0, 1.0)]),
                 (2 * d, 3 * d, None, [(3, 0, 1.0), (4, 0, 1.0)])])
            o_p = _sb_prompt(qb, kb, vb, bp, seq)
            o_s = _sb_sample(qb, kb, vb, cache_sb_k[j].reshape(bs, -1, d), cache_sb_v[j].reshape(bs, -1, d), bs, off_s)
            shp = (SB_HEADS, SB_DH)
            outs.setdefault("sb", []).append((
                k[:n_p].reshape(bp, seq, *shp), v[:n_p].reshape(bp, seq, *shp),
                k[n_p:].reshape(bs, dec, *shp), v[n_p:].reshape(bs, dec, *shp)))
            w_out = sb_w_out[j]
        x = _outproj(x, jnp.concatenate([o_p, o_s], axis=0), w_out.astype(BF16))
        x = _ffn(x, ffn2_norm[i], ffn2_w_gu[i].astype(BF16), ffn2_w_down[i].astype(BF16))
    stack = lambda key, idx: jnp.stack([t[idx] for t in outs[key]])
    return (x[:n_p].reshape(bp, seq, d), x[n_p:].reshape(bs, dec, d),
            stack("gla", 0), stack("gla", 1),
            stack("band", 0), stack("band", 1), stack("band", 2), stack("band", 3),
            stack("gdn", 0), stack("gdn", 1), stack("gdn", 2), stack("gdn", 3),
            stack("sb", 0), stack("sb", 1), stack("sb", 2), stack("sb", 3))
```

```python
import functools

import jax
import jax.numpy as jnp
import numpy as np
from jax import lax
from jax.experimental import pallas as pl
from jax.experimental.pallas import tpu as pltpu

F32 = jnp.float32
BF16 = jnp.bfloat16

D_MODEL = 1024
CHUNK = 64
EPS = 1e-6
D_FF = 2816
GLA_HEADS, GLA_DK, GLA_DV, GLA_RANK, GLA_TAU = 4, 128, 256, 16, 16.0
BAND_HEADS, BAND_DH, LEFT_CHUNKS, REL_CLIP = 16, 64, 8, 256
BAND_WINDOW = LEFT_CHUNKS * CHUNK
BAND_SPAN = BAND_WINDOW + CHUNK
GDN_HEADS, GDN_DK, GDN_DV, CONV_W = 8, 128, 128, 4
GDN_CONV_CH = 2 * GDN_HEADS * GDN_DK + GDN_HEADS * GDN_DV
SB_HEADS, SB_DH = 16, 64
LOG2E = 1.4426950408889634

LANES = 128
SUBLANES = 8
VMEM_LIMIT = 56 * 1024 * 1024
ROW_TILE = 512
SB_TK = 256


def _params(*sem):
    return pltpu.CompilerParams(dimension_semantics=sem, vmem_limit_bytes=VMEM_LIMIT)


def _dot(a, b):
    return jnp.dot(a, b, preferred_element_type=F32)


def _dot_nt(a, b):
    return lax.dot_general(a, b, (((1,), (1,)), ((), ())), preferred_element_type=F32)


def _dot_tn(a, b):
    return lax.dot_general(a, b, (((0,), (0,)), ((), ())), preferred_element_type=F32)


def _split3(x):
    h1 = x.astype(BF16)
    r1 = x - h1.astype(F32)
    h2 = r1.astype(BF16)
    h3 = (r1 - h2.astype(F32)).astype(BF16)
    return h1, h2, h3


def _tri_dot(tri, x):
    h1, h2, h3 = _split3(x)
    return _dot(tri, h1) + _dot(tri, h2) + _dot(tri, h3)


def _dot_f32(a, b):
    a1 = a.astype(BF16)
    a2 = (a - a1.astype(F32)).astype(BF16)
    b1 = b.astype(BF16)
    b2 = (b - b1.astype(F32)).astype(BF16)
    return _dot(a1, b1) + (_dot(a1, b2) + _dot(a2, b1))


def _sigmoid(x):
    return 1.0 / (1.0 + jnp.exp(-x))


def _silu(x):
    return x * _sigmoid(x)


def _softplus(x):
    return jnp.maximum(x, 0.0) + jnp.log1p(jnp.exp(-jnp.abs(x)))


def _log_sigmoid(x):
    return -_softplus(-x)


def _rms(x, g):
    return x * lax.rsqrt(jnp.mean(x * x, axis=-1, keepdims=True) + EPS) * g


def _iota2(shape, dim):
    return lax.broadcasted_iota(jnp.int32, shape, dim)


def _ffn_kernel(*refs, fused, prompt_steps):
    if fused:
        x_ref, ap_ref, as_ref, wo_ref, g_ref, wg_ref, wu_ref, wd_ref, o_ref = refs
        a = jnp.where(pl.program_id(0) < prompt_steps, ap_ref[...], as_ref[...])
        x = x_ref[...] + _dot(a, wo_ref[...])
    else:
        x_ref, g_ref, wg_ref, wu_ref, wd_ref, o_ref = refs
        x = x_ref[...]
    xn = _rms(x, g_ref[...]).astype(BF16)
    g = _dot(xn, wg_ref[...])
    u = _dot(xn, wu_ref[...])
    h = (_silu(g) * u).astype(BF16)
    o_ref[...] = x + 0.5 * _dot(h, wd_ref[...])


def _ffn(x, gain, w_gu, w_down, mix=None):
    t, d = x.shape
    once = dict(pipeline_mode=pl.Buffered(1))
    row = pl.BlockSpec((ROW_TILE, d), lambda i: (i, 0))
    in_specs = [row]
    args = [x]
    prompt_steps = 0
    if mix is not None:
        prompt_steps = mix[0].shape[0] // ROW_TILE
        in_specs += [pl.BlockSpec((ROW_TILE, d), lambda i: (jnp.minimum(i, prompt_steps - 1), 0)),
                     pl.BlockSpec((ROW_TILE, d), lambda i: (jnp.maximum(i - prompt_steps, 0), 0)),
                     pl.BlockSpec((d, d), lambda i: (0, 0), **once)]
        args += list(mix)
    in_specs += [
        pl.BlockSpec((1, d), lambda i: (0, 0)),
        pl.BlockSpec((d, D_FF), lambda i: (0, 0), **once),
        pl.BlockSpec((d, D_FF), lambda i: (0, 1), **once),
        pl.BlockSpec((D_FF, d), lambda i: (0, 0), **once),
    ]
    return pl.pallas_call(
        functools.partial(_ffn_kernel, fused=mix is not None, prompt_steps=prompt_steps),
        out_shape=jax.ShapeDtypeStruct((t, d), F32),
        grid=(t // ROW_TILE,),
        in_specs=in_specs,
        out_specs=row,
        compiler_params=_params("parallel"),
        name="ffn",
    )(*args, gain.reshape(1, d), w_gu, w_gu, w_down)


def _pair_head_norm(x, gain2):
    low = _iota2(x.shape, 1) < BAND_DH
    sq = x * x
    s_lo = jnp.sum(jnp.where(low, sq, 0.0), axis=-1, keepdims=True)
    s_hi = jnp.sum(jnp.where(low, 0.0, sq), axis=-1, keepdims=True)
    ms = jnp.where(low, s_lo, s_hi) * (1.0 / BAND_DH)
    return x * lax.rsqrt(ms + EPS) * gain2


def _proj_kernel(*refs, segs, n_gain):
    x_ref, g_ref, w_ref = refs[:3]
    gains = refs[3:3 + n_gain]
    outs = refs[3 + n_gain:]
    xn = _rms(x_ref[...], g_ref[...]).astype(BF16)
    for c0, c1, gain_idx, dsts in segs:
        y = _dot(xn, w_ref[:, c0:c1])
        if gain_idx is not None:
            y = jnp.concatenate([_pair_head_norm(y[:, s:s + LANES], gains[gain_idx][...])
                                 for s in range(0, c1 - c0, LANES)], axis=1)
        for out_idx, off, scale in dsts:
            o_ref = outs[out_idx]
            o_ref[:, off:off + c1 - c0] = (y if scale == 1.0 else y * scale).astype(o_ref.dtype)


def _proj(x, gain, w, out_defs, segs, gains=(), rows=None):
    d = x.shape[1]
    r0, r1 = rows or (0, x.shape[0])
    first = r0 // ROW_TILE
    const = lambda shape: pl.BlockSpec(shape, lambda i: (0, 0))
    return pl.pallas_call(
        functools.partial(_proj_kernel, segs=tuple(segs), n_gain=len(gains)),
        out_shape=tuple(jax.ShapeDtypeStruct((r1 - r0, width), dt) for width, dt in out_defs),
        grid=((r1 - r0) // ROW_TILE,),
        in_specs=[pl.BlockSpec((ROW_TILE, d), lambda i: (first + i, 0)), const((1, d)),
                  pl.BlockSpec(w.shape, lambda i: (0, 0), pipeline_mode=pl.Buffered(1))]
        + [const((1, LANES))] * len(gains),
        out_specs=tuple(pl.BlockSpec((ROW_TILE, width), lambda i: (i, 0)) for width, _ in out_defs),
        compiler_params=_params("parallel"),
        name="proj",
    )(x, gain.reshape(1, d), w, *[jnp.tile(g, 2).reshape(1, LANES) for g in gains])


def _col_segs(n, dsts_of):
    return [(c, min(c + D_MODEL, n), None, dsts_of(c)) for c in range(0, n, D_MODEL)]


def _gla_kernel(*refs, has_init):
    if has_init:
        (q_ref, k_ref, v_ref, r_ref, gl_ref, wg2_ref, bg_ref, on_ref, s0_ref,
         o_ref, sfin_ref, st_ref) = refs
    else:
        (q_ref, k_ref, v_ref, r_ref, gl_ref, wg2_ref, bg_ref, on_ref,
         o_ref, sfin_ref, st_ref) = refs
    c = pl.program_id(1)

    @pl.when(c == 0)
    def _():
        for h in range(GLA_HEADS):
            if has_init:
                st_ref[h] = s0_ref[0, h].T
            else:
                st_ref[h] = jnp.zeros((GLA_DV, GLA_DK), F32)

    incl = _iota2((CHUNK, CHUNK), 0) >= _iota2((CHUNK, CHUNK), 1)
    tri = jnp.where(incl, 1.0, 0.0).astype(BF16)
    x = _dot(gl_ref[...].astype(BF16), wg2_ref[...]) + bg_ref[...]
    log_a = _log_sigmoid(x) * (1.0 / GLA_TAU)
    b = _tri_dot(tri, log_a)
    b_last = b[CHUNK - 1:CHUNK, :]
    q = q_ref[...] * (GLA_DK ** -0.5)
    k = k_ref[...]
    q_e = (q * jnp.exp(b)).astype(BF16)
    k_e = (k * jnp.exp(-b)).astype(BF16)
    k_l = (k * jnp.exp(b_last - b)).astype(BF16)
    d_last = jnp.exp(b_last)
    gain = on_ref[...]
    for h in range(GLA_HEADS):
        ks = slice(h * GLA_DK, (h + 1) * GLA_DK)
        vs = slice(h * GLA_DV, (h + 1) * GLA_DV)
        v = v_ref[:, vs].astype(BF16)
        st = st_ref[h]
        att = jnp.where(incl, _dot_nt(q_e[:, ks], k_e[:, ks]), 0.0)
        o = _dot(att.astype(BF16), v) + _dot_nt(q_e[:, ks], st.astype(BF16))
        st_ref[h] = st * d_last[:, ks] + _dot_tn(v, k_l[:, ks])
        o = _rms(o, gain) * _silu(r_ref[:, vs])
        o_ref[:, vs] = o.astype(BF16)

    @pl.when(c == pl.num_programs(1) - 1)
    def _():
        for h in range(GLA_HEADS):
            sfin_ref[0, h] = st_ref[h].T


def _gla(proj, w_gate2, b_gate, onorm, s0, nb, nc, row_off):
    qk = GLA_HEADS * GLA_DK
    vv = GLA_HEADS * GLA_DV
    row = lambda b, c: row_off + b * nc + c
    in_specs = [
        pl.BlockSpec((CHUNK, qk), lambda b, c: (row(b, c), 0)),
        pl.BlockSpec((CHUNK, qk), lambda b, c: (row(b, c), 1)),
        pl.BlockSpec((CHUNK, vv), lambda b, c: (row(b, c), 1)),
        pl.BlockSpec((CHUNK, vv), lambda b, c: (row(b, c), 2)),
        pl.BlockSpec((CHUNK, LANES), lambda b, c: (row(b, c), (2 * qk + 2 * vv) // LANES)),
        pl.BlockSpec((LANES, qk), lambda b, c: (0, 0)),
        pl.BlockSpec((1, qk), lambda b, c: (0, 0)),
        pl.BlockSpec((1, GLA_DV), lambda b, c: (0, 0)),
    ]
    args = [proj, proj, proj, proj, proj, w_gate2, b_gate, onorm]
    if s0 is not None:
        in_specs.append(pl.BlockSpec((1, GLA_HEADS, GLA_DK, GLA_DV), lambda b, c: (b, 0, 0, 0)))
        args.append(s0)
    return pl.pallas_call(
        functools.partial(_gla_kernel, has_init=s0 is not None),
        out_shape=(jax.ShapeDtypeStruct((nb * nc * CHUNK, vv), BF16),
                   jax.ShapeDtypeStruct((nb, GLA_HEADS, GLA_DK, GLA_DV), F32)),
        grid=(nb, nc),
        in_specs=in_specs,
        out_specs=(pl.BlockSpec((CHUNK, vv), lambda b, c: (b * nc + c, 0)),
                   pl.BlockSpec((1, GLA_HEADS, GLA_DK, GLA_DV), lambda b, c: (b, 0, 0, 0))),
        scratch_shapes=[pltpu.VMEM((GLA_HEADS, GLA_DV, GLA_DK), F32)],
        compiler_params=_params("parallel", "arbitrary"),
        name="gla",
    )(*args)


BIAS_WIDTH = 640


def _band_bias_kernel(w_ref, o_ref):
    for h in range(BAND_HEADS):
        x = jnp.broadcast_to(w_ref[h:h + 1, :], (CHUNK, BIAS_WIDTH))
        rolled = pltpu.roll(x, 0, 1, stride=1, stride_axis=0)
        o_ref[h // 2, (h % 2) * CHUNK:(h % 2 + 1) * CHUNK, :] = rolled[:, 0:BAND_SPAN]


def _band_bias(rel_bias):
    c = jnp.arange(BIAS_WIDTH)
    n = jnp.where(c <= BAND_SPAN, c, c - BIAS_WIDTH)
    w = rel_bias[:, jnp.clip(BAND_WINDOW - n, -REL_CLIP, REL_CLIP) + REL_CLIP]
    return pl.pallas_call(
        _band_bias_kernel,
        out_shape=jax.ShapeDtypeStruct((BAND_HEADS // 2, 2 * CHUNK, BAND_SPAN), F32),
        name="band_bias",
    )(w)


def _band_attn_kernel(q_ref, kp_ref, kc_ref, vp_ref, vc_ref, bias_ref, o_ref, kall_ref, vall_ref,
                      *, nq, prompt):
    if prompt:
        kall_ref[0:BAND_WINDOW] = kp_ref[...]
        vall_ref[0:BAND_WINDOW] = vp_ref[...]
    else:
        kall_ref[0:BAND_WINDOW] = kp_ref[0].astype(BF16)
        vall_ref[0:BAND_WINDOW] = vp_ref[0].astype(BF16)
    kall_ref[BAND_WINDOW:] = kc_ref[...]
    vall_ref[BAND_WINDOW:] = vc_ref[...]
    first_pos = (pl.program_id(1) - 1) * BAND_WINDOW if prompt else 0
    key_idx = _iota2((2 * CHUNK, BAND_SPAN), 1)
    pairs = range(D_MODEL // LANES)
    lanes = [slice(p * LANES, (p + 1) * LANES) for p in pairs]

    def chunk(t, carry):
        r0 = pl.multiple_of(t * CHUNK, CHUNK)
        qs = [_stack_pair(q_ref[pl.ds(r0, CHUNK), ls]) for ls in lanes]
        s = [_dot_nt(qs[p], kall_ref[pl.ds(r0, BAND_SPAN), lanes[p]]) * (BAND_DH ** -0.5) + bias_ref[p]
             for p in pairs]
        if prompt:
            visible = first_pos + r0 + key_idx >= 0
            s = [jnp.where(visible, x, -jnp.inf) for x in s]
        e = [jnp.exp(x - jnp.max(x, axis=-1, keepdims=True)) for x in s]
        prob = [(x / jnp.sum(x, axis=-1, keepdims=True)).astype(BF16) for x in e]
        pv = [_dot(prob[p], vall_ref[pl.ds(r0, BAND_SPAN), lanes[p]]) for p in pairs]
        for p in pairs:
            o_ref[pl.ds(r0, CHUNK), lanes[p]] = _unstack_pair(pv[p]).astype(BF16)
        return carry

    lax.fori_loop(0, nq, chunk, 0)


def _band_attn_prompt(qn, knb, vb, bias, nb, seq):
    d = D_MODEL
    blk = BAND_WINDOW
    nblk = seq // blk
    cur = pl.BlockSpec((blk, d), lambda b, i: (b * nblk + i, 0))
    prev = pl.BlockSpec((blk, d), lambda b, i: (b * nblk + jnp.maximum(i - 1, 0), 0))
    return pl.pallas_call(
        functools.partial(_band_attn_kernel, nq=blk // CHUNK, prompt=True),
        out_shape=jax.ShapeDtypeStruct((nb * seq, d), BF16),
        grid=(nb, nblk),
        in_specs=[cur, prev, cur, prev, cur,
                  pl.BlockSpec(bias.shape, lambda b, i: (0, 0, 0))],
        out_specs=cur,
        scratch_shapes=[pltpu.VMEM((2 * blk, d), BF16), pltpu.VMEM((2 * blk, d), BF16)],
        compiler_params=_params("parallel", "arbitrary"),
        name="band_attn_prompt",
    )(qn, knb, knb, vb, vb, bias)


def _band_attn_sample(qn, knb, vb, cache_k, cache_v, bias, nb, row_off):
    d = D_MODEL
    new = pl.BlockSpec((CHUNK, d), lambda b: (row_off + b, 0))
    old = pl.BlockSpec((1, BAND_WINDOW, d), lambda b: (b, 0, 0))
    return pl.pallas_call(
        functools.partial(_band_attn_kernel, nq=1, prompt=False),
        out_shape=jax.ShapeDtypeStruct((nb * CHUNK, d), BF16),
        grid=(nb,),
        in_specs=[new, old, new, old, new, pl.BlockSpec(bias.shape, lambda b: (0, 0, 0))],
        out_specs=pl.BlockSpec((CHUNK, d), lambda b: (b, 0)),
        scratch_shapes=[pltpu.VMEM((BAND_SPAN, d), BF16), pltpu.VMEM((BAND_SPAN, d), BF16)],
        compiler_params=_params("parallel"),
        name="band_attn_sample",
    )(qn, cache_k, knb, cache_v, vb, bias)


def _gdn_prep_kernel(*refs, has_init):
    if has_init:
        x_ref, raw_ref, cw_ref, par_ref, c0_ref, q_ref, k_ref, v_ref, bg_ref, carry_ref = refs
    else:
        x_ref, raw_ref, cw_ref, par_ref, q_ref, k_ref, v_ref, bg_ref, carry_ref = refs
    tb = x_ref.shape[0]

    @pl.when(pl.program_id(1) == 0)
    def _():
        if has_init:
            carry_ref[...] = c0_ref[0]
        else:
            carry_ref[...] = jnp.zeros_like(carry_ref)

    x = x_ref[...]
    w = cw_ref[...]
    x8 = x[0:SUBLANES]
    c8 = carry_ref[...]
    row8 = _iota2(x8.shape, 0)
    y = x * w[CONV_W - 1:CONV_W]
    y8 = x8 * w[CONV_W - 1:CONV_W]
    for s in range(1, CONV_W):
        ws = w[CONV_W - 1 - s:CONV_W - s]
        y = y + pltpu.roll(x, s, 0) * ws
        y8 = y8 + jnp.where(row8 < s, pltpu.roll(c8, s, 0), pltpu.roll(x8, s, 0)) * ws
    carry_ref[...] = x[tb - SUBLANES:tb]
    if tb > SUBLANES:
        y = jnp.concatenate([y8, y[SUBLANES:]], axis=0)
    else:
        y = y8
    y = _silu(y)
    qk = GDN_HEADS * GDN_DK
    for h in range(GDN_HEADS):
        qh = y[:, h * GDN_DK:(h + 1) * GDN_DK]
        kh = y[:, qk + h * GDN_DK:qk + (h + 1) * GDN_DK]
        q_ref[h] = qh * lax.rsqrt(jnp.sum(qh * qh, axis=-1, keepdims=True) + EPS)
        k_ref[h] = kh * lax.rsqrt(jnp.sum(kh * kh, axis=-1, keepdims=True) + EPS)
        v_ref[h] = y[:, 2 * qk + h * GDN_DV:2 * qk + (h + 1) * GDN_DV]
    raw = raw_ref[...]
    beta = _sigmoid(raw)
    g = par_ref[0:1, :] * _softplus(raw + par_ref[1:2, :])
    bg_ref[...] = jnp.where(_iota2(raw.shape, 1) < GDN_HEADS, beta, g)


def _gdn_prep(proj, conv_w, par, c0, nb, seq, row_off, tb):
    nblk = seq // tb
    base = row_off // tb
    row = lambda b, i: base + b * nblk + i
    in_specs = [
        pl.BlockSpec((tb, GDN_CONV_CH), lambda b, i: (row(b, i), 0)),
        pl.BlockSpec((tb, LANES), lambda b, i: (row(b, i), (GDN_CONV_CH + GDN_HEADS * GDN_DV) // LANES)),
        pl.BlockSpec((CONV_W, GDN_CONV_CH), lambda b, i: (0, 0)),
        pl.BlockSpec((2, LANES), lambda b, i: (0, 0)),
    ]
    args = [proj, proj, conv_w, par]
    if c0 is not None:
        in_specs.append(pl.BlockSpec((1, SUBLANES, GDN_CONV_CH), lambda b, i: (b, 0, 0)))
        args.append(c0)
    n = nb * seq
    head = pl.BlockSpec((GDN_HEADS, tb, GDN_DK), lambda b, i: (0, b * nblk + i, 0))
    return pl.pallas_call(
        functools.partial(_gdn_prep_kernel, has_init=c0 is not None),
        out_shape=(jax.ShapeDtypeStruct((GDN_HEADS, n, GDN_DK), F32),
                   jax.ShapeDtypeStruct((GDN_HEADS, n, GDN_DK), F32),
                   jax.ShapeDtypeStruct((GDN_HEADS, n, GDN_DV), F32),
                   jax.ShapeDtypeStruct((n, LANES), F32)),
        grid=(nb, nblk),
        in_specs=in_specs,
        out_specs=(head, head, head, pl.BlockSpec((tb, LANES), lambda b, i: (b * nblk + i, 0))),
        scratch_shapes=[pltpu.VMEM((SUBLANES, GDN_CONV_CH), F32)],
        compiler_params=_params("parallel", "arbitrary"),
        name="gdn_prep",
    )(*args)


def _gdn_kernel(*refs, has_init):
    if has_init:
        q_ref, k_ref, v_ref, bg_ref, gate_ref, on_ref, s0_ref, o_ref, sfin_ref, s_ref = refs
    else:
        q_ref, k_ref, v_ref, bg_ref, gate_ref, on_ref, o_ref, sfin_ref, s_ref = refs
    c = pl.program_id(1)

    @pl.when(c == 0)
    def _():
        if has_init:
            s_ref[...] = s0_ref[0]
        else:
            s_ref[...] = jnp.zeros_like(s_ref)

    ri = _iota2((CHUNK, CHUNK), 0)
    ci = _iota2((CHUNK, CHUNK), 1)
    incl = ri >= ci
    strict = ri > ci
    tri = jnp.where(incl, 1.0, 0.0).astype(BF16)
    eye = jnp.where(ri == ci, 1.0, 0.0)
    bg = bg_ref[...]
    cs = _tri_dot(tri, bg)
    cs_t = cs.T
    gain = on_ref[...]
    heads = range(GDN_HEADS)
    beta = [bg[:, h:h + 1] for h in heads]
    gcol = [cs[:, GDN_HEADS + h:GDN_HEADS + h + 1] for h in heads]
    glast = [g[CHUNK - 1:CHUNK, :] for g in gcol]
    gam = [jnp.exp(jnp.where(incl, gcol[h] - cs_t[GDN_HEADS + h:GDN_HEADS + h + 1, :], -jnp.inf)) for h in heads]
    egc = [jnp.exp(g) for g in gcol]
    k = [k_ref[h] for h in heads]
    k16 = [x.astype(BF16) for x in k]
    kb = [k[h] * beta[h] for h in heads]
    m = [jnp.where(strict, _dot_nt(kb[h].astype(BF16), k16[h]) * gam[h], 0.0) for h in heads]
    d = [eye] * GDN_HEADS
    for lvl in range(6):
        sub = (((ri >> lvl) & 1) == 1) & (((ci >> lvl) & 1) == 0) & ((ri >> (lvl + 1)) == (ci >> (lvl + 1)))
        y = [_dot_f32(d[h], jnp.where(sub, m[h], 0.0)) for h in heads]
        x = [_dot_f32(y[h], d[h]) for h in heads]
        d = [d[h] - x[h] for h in heads]
    t16 = [t.astype(BF16) for t in d]
    u = [_dot(t16[h], (v_ref[h] * beta[h]).astype(BF16)) for h in heads]
    w = [_dot(t16[h], (kb[h] * egc[h]).astype(BF16)) for h in heads]
    q = [q_ref[h] * (GDN_DK ** -0.5) for h in heads]
    att = [_dot_nt(q[h].astype(BF16), k16[h]) * gam[h] for h in heads]
    q_e = [(q[h] * egc[h]).astype(BF16) for h in heads]
    k_l = [(k[h] * jnp.exp(glast[h] - gcol[h])).astype(BF16) for h in heads]
    s = [s_ref[h] for h in heads]
    s16 = [x.astype(BF16) for x in s]
    v_new = [u[h] - _dot(w[h].astype(BF16), s16[h]) for h in heads]
    v16 = [x.astype(BF16) for x in v_new]
    o = [_dot(q_e[h], s16[h]) + _dot(att[h].astype(BF16), v16[h]) for h in heads]
    for h in heads:
        s_ref[h] = s[h] * jnp.exp(glast[h]) + _dot_tn(k_l[h], v16[h])
        vs = slice(h * GDN_DV, (h + 1) * GDN_DV)
        o_ref[:, vs] = (_rms(o[h], gain) * _silu(gate_ref[:, vs])).astype(BF16)

    @pl.when(c == pl.num_programs(1) - 1)
    def _():
        sfin_ref[0] = s_ref[...]


def _gdn(qh, kh, vh, bg, proj, onorm, s0, nb, nc, row_off):
    vv = GDN_HEADS * GDN_DV
    head = pl.BlockSpec((GDN_HEADS, CHUNK, GDN_DK), lambda b, c: (0, b * nc + c, 0))
    in_specs = [
        head, head, head,
        pl.BlockSpec((CHUNK, LANES), lambda b, c: (b * nc + c, 0)),
        pl.BlockSpec((CHUNK, vv), lambda b, c: (row_off + b * nc + c, GDN_CONV_CH // vv)),
        pl.BlockSpec((1, GDN_DV), lambda b, c: (0, 0)),
    ]
    args = [qh, kh, vh, bg, proj, onorm]
    state = pl.BlockSpec((1, GDN_HEADS, GDN_DK, GDN_DV), lambda b, c: (b, 0, 0, 0))
    if s0 is not None:
        in_specs.append(state)
        args.append(s0)
    return pl.pallas_call(
        functools.partial(_gdn_kernel, has_init=s0 is not None),
        out_shape=(jax.ShapeDtypeStruct((nb * nc * CHUNK, vv), BF16),
                   jax.ShapeDtypeStruct((nb, GDN_HEADS, GDN_DK, GDN_DV), F32)),
        grid=(nb, nc),
        in_specs=in_specs,
        out_specs=(pl.BlockSpec((CHUNK, vv), lambda b, c: (b * nc + c, 0)), state),
        scratch_shapes=[pltpu.VMEM((GDN_HEADS, GDN_DK, GDN_DV), F32)],
        compiler_params=_params("parallel", "arbitrary"),
        name="gdn",
    )(*args)


def _neg_abs(x):
    bits = lax.bitcast_convert_type(x, jnp.uint32) | jnp.uint32(0x80000000)
    return lax.bitcast_convert_type(bits, F32)


def _stack_pair(q2):
    low = _iota2(q2.shape, 1) < SB_DH
    return jnp.concatenate([jnp.where(low, q2, 0), jnp.where(low, 0, q2)], axis=0)


def _unstack_pair(acc):
    tq = acc.shape[0] // 2
    low = _iota2((tq, LANES), 1) < SB_DH
    return jnp.where(low, acc[:tq], acc[tq:])


def _upper(tk):
    return jnp.where(_iota2((tk, tk), 0) > _iota2((tk, tk), 1), 1.0, 0.0).astype(BF16)


def _diag_mask(tq, tk):
    r = _iota2((2 * tq, tk), 0)
    r = jnp.where(r >= tq, r - tq, r)
    return _iota2((2 * tq, tk), 1) < r


SB_NEUTRAL = 1e30
SB_LAG = 4


def _sb_log_terms(zn, mask):
    l1m = jnp.minimum(zn, 0.0) - jnp.log2(1.0 + jnp.exp2(_neg_abs(zn)))
    log_b = l1m - zn
    if mask is not None:
        l1m = jnp.where(mask, l1m, 0.0)
        log_b = jnp.where(mask, log_b, -SB_NEUTRAL)
    return l1m, log_b


def _sb_direct(qs, k2, v2, mask):
    l1m, log_b = _sb_log_terms(_dot_nt(qs, k2), mask)
    a = jnp.exp2(log_b + _dot(l1m.astype(BF16), _upper(k2.shape[0])))
    return jnp.sum(l1m, axis=-1, keepdims=True), _dot(a.astype(BF16), v2)


def _sb_pipeline(nhalf, nchain, q_of, k_of, v_of, st_of, upper, scratch, mask):
    z_ref, hi_ref, logb_ref, tot_ref, cs_ref, a_ref, run_ref, acc_ref = scratch
    logb_ref[...] = jnp.full(logb_ref.shape, -SB_NEUTRAL, F32)
    tot_ref[...] = jnp.zeros(tot_ref.shape, F32)
    for c in range(nchain):
        z_ref[2 * c + 1] = jnp.full(z_ref.shape[1:], SB_NEUTRAL, F32)
        hi_ref[2 * c + 1] = jnp.zeros(hi_ref.shape[1:], BF16)
        cs_ref[2 * c + 1] = jnp.zeros(cs_ref.shape[1:], F32)
        a_ref[2 * c + 1] = jnp.zeros(a_ref.shape[1:], BF16)

    def half(h, p):
        for c in range(nchain):
            rows = st_of(h - 4, c)
            acc_ref[rows, :] += _dot(a_ref[2 * c + 1 - p], v_of(h - 4, c))
        for c in range(nchain):
            rows = st_of(h - 3, c)
            run = run_ref[rows, :]
            a_ref[2 * c + p] = jnp.exp2(cs_ref[2 * c + 1 - p] + (logb_ref[2 * c + p] + run)).astype(BF16)
            run_ref[rows, :] = run + tot_ref[2 * c + p]
        for c in range(nchain):
            cs_ref[2 * c + p] = _dot(hi_ref[2 * c + 1 - p], upper)
        for c in range(nchain):
            l1m, logb_ref[2 * c + p] = _sb_log_terms(z_ref[2 * c + 1 - p], mask)
            hi_ref[2 * c + p] = l1m.astype(BF16)
            tot_ref[2 * c + p] = jnp.sum(l1m, axis=-1, keepdims=True)
        for c in range(nchain):
            z_ref[2 * c + p] = _dot_nt(q_of(h, c), k_of(h, c))

    def body(i, _):
        half(2 * i, 0)
        half(2 * i + 1, 1)
        return 0

    assert nhalf % 2 == 0
    lax.fori_loop(0, nhalf // 2, body, 0)


def _sb_rings(nchain, rows, tk):
    slots = 2 * nchain
    return [pltpu.VMEM((slots, rows, tk), F32), pltpu.VMEM((slots, rows, tk), BF16),
            pltpu.VMEM((slots, rows, tk), F32), pltpu.VMEM((slots, rows, 1), F32),
            pltpu.VMEM((slots, rows, tk), F32), pltpu.VMEM((slots, rows, tk), BF16)]


def _sb_prompt_items(nq):
    diag = [(i, i) for i in range(nq)]
    off = [(j, i) for j in range(nq - 2, -1, -1) for i in range(j + 1, nq)]
    pad = [(0, nq)] * (2 * SB_LAG)
    assert len(diag) % 4 == 0 and len(off) % 4 == 0
    rows = []
    for items in (diag, off):
        padded = pad + items + pad
        rows += [[j for j, _ in padded], [i for _, i in padded]]
    width = max(len(r) for r in rows)
    tab = np.zeros((4, width), np.int32)
    for r, row in enumerate(rows):
        tab[r, :len(row)] = row
    return jnp.asarray(tab), len(diag), len(off)


def _sb_prompt_kernel(tab_ref, q_ref, k_ref, v_ref, o_ref, qs_ref, run_ref, acc_ref, *rings, nq, ndiag, noff):
    tk = SB_TK
    rows = 2 * tk

    def stack(i, _):
        qs_ref[pl.ds(pl.multiple_of(i * rows, rows), rows), :] = _stack_pair(
            q_ref[pl.ds(pl.multiple_of(i * tk, tk), tk), :])
        return 0

    lax.fori_loop(0, nq, stack, 0)
    qs_ref[nq * rows:(nq + 1) * rows, :] = jnp.zeros((rows, LANES), BF16)
    run_ref[...] = jnp.zeros(run_ref.shape, F32)
    acc_ref[...] = jnp.zeros(acc_ref.shape, F32)
    upper = _upper(tk)
    scratch = list(rings) + [run_ref, acc_ref]
    for row, nitems, mask in ((0, ndiag, _diag_mask(tk, tk)), (2, noff, None)):
        col = lambda hh, c: 2 * hh + c + 2 * SB_LAG
        q_rows = lambda hh, c: pl.ds(pl.multiple_of(tab_ref[row + 1, col(hh, c)] * rows, rows), rows)
        k_rows = lambda hh, c: pl.ds(pl.multiple_of(tab_ref[row, col(hh, c)] * tk, tk), tk)
        _sb_pipeline(nitems // 2 + SB_LAG, 2,
                     lambda hh, c: qs_ref[q_rows(hh, c), :],
                     lambda hh, c: k_ref[k_rows(hh, c), :],
                     lambda hh, c: v_ref[k_rows(hh, c), :],
                     q_rows, upper, scratch, mask)

    def emit(i, _):
        o_ref[pl.ds(pl.multiple_of(i * tk, tk), tk), :] = _unstack_pair(
            acc_ref[pl.ds(pl.multiple_of(i * rows, rows), rows), :]).astype(BF16)
        return 0

    lax.fori_loop(0, nq, emit, 0)


def _sb_prompt(qb, kb, vb, nb, seq):
    d = D_MODEL
    npair = d // LANES
    nq = seq // SB_TK
    rows = 2 * SB_TK
    tab, ndiag, noff = _sb_prompt_items(nq)
    col = pl.BlockSpec((seq, LANES), lambda b, p, tab: (b, p))
    return pl.pallas_call(
        functools.partial(_sb_prompt_kernel, nq=nq, ndiag=ndiag, noff=noff),
        out_shape=jax.ShapeDtypeStruct((nb * seq, d), BF16),
        grid_spec=pltpu.PrefetchScalarGridSpec(
            num_scalar_prefetch=1,
            grid=(nb, npair),
            in_specs=[col, col, col],
            out_specs=col,
            scratch_shapes=[pltpu.VMEM(((nq + 1) * rows, LANES), BF16), pltpu.VMEM(((nq + 1) * rows, 1), F32),
                            pltpu.VMEM(((nq + 1) * rows, LANES), F32)] + _sb_rings(2, rows, SB_TK)),
        compiler_params=_params("parallel", "arbitrary"),
        name="sb_prompt",
    )(tab, qb, kb, vb)


def _sb_sample_kernel(q_ref, kn_ref, vn_ref, kc_ref, vc_ref, o_ref, run_ref, acc_ref, *rings):
    tq, tk = CHUNK, SB_TK
    rows = 2 * tq
    nblk = kc_ref.shape[1] // tk
    lanes = lambda c: slice(c * LANES, (c + 1) * LANES)
    qs = [_stack_pair(q_ref[:, lanes(c)]) for c in range(2)]
    for c in range(2):
        run, acc = _sb_direct(qs[c], kn_ref[:, lanes(c)], vn_ref[:, lanes(c)], _diag_mask(tq, tq))
        run_ref[c * rows:(c + 1) * rows, :] = run
        acc_ref[c * rows:(c + 1) * rows, :] = acc

    def k_rows(hh):
        return pl.ds(pl.multiple_of(jnp.clip(nblk - 1 - hh, 0, nblk - 1) * tk, tk), tk)

    def st_of(hh, c):
        real = (hh >= 0) & (hh < nblk)
        return pl.ds(pl.multiple_of(jnp.where(real, c, 2) * rows, rows), rows)

    _sb_pipeline(nblk + SB_LAG, 2,
                 lambda hh, c: qs[c],
                 lambda hh, c: kc_ref[0, k_rows(hh), lanes(c)].astype(BF16),
                 lambda hh, c: vc_ref[0, k_rows(hh), lanes(c)].astype(BF16),
                 st_of, _upper(tk), list(rings) + [run_ref, acc_ref], None)
    for c in range(2):
        o_ref[:, lanes(c)] = _unstack_pair(acc_ref[c * rows:(c + 1) * rows, :]).astype(BF16)


def _sb_sample(qb, kb, vb, cache_k, cache_v, nb, row_off):
    d = D_MODEL
    wide = 2 * LANES
    ngrp = d // wide
    past = cache_k.shape[1]
    rows = 2 * CHUNK
    new = pl.BlockSpec((CHUNK, wide), lambda b, g: (row_off + b, g))
    old = pl.BlockSpec((1, past, wide), lambda b, g: (b, 0, g))
    return pl.pallas_call(
        _sb_sample_kernel,
        out_shape=jax.ShapeDtypeStruct((nb * CHUNK, d), BF16),
        grid=(nb, ngrp),
        in_specs=[new, new, new, old, old],
        out_specs=pl.BlockSpec((CHUNK, wide), lambda b, g: (b, g)),
        scratch_shapes=[pltpu.VMEM((3 * rows, 1), F32), pltpu.VMEM((3 * rows, LANES), F32)]
        + _sb_rings(2, rows, SB_TK),
        compiler_params=_params("parallel", "parallel"),
        name="sb_sample",
    )(qb, kb, vb, cache_k, cache_v)


def _pad_cols(w, n):
    return jnp.pad(w, ((0, 0), (0, n - w.shape[1])))


def kernel(x_prompt, x_sample, state_gla, cache_band_k, cache_band_v, state_gdn, state_gdn_conv,
           cache_sb_k, cache_sb_v, ffn1_norm, ffn1_w_gu, ffn1_w_down, mix_norm, ffn2_norm,
           ffn2_w_gu, ffn2_w_down, gla_w_in, gla_w_gate2, gla_b_gate, gla_onorm, gla_w_out,
           band_w_in, band_q_norm, band_k_norm, band_rel_bias, band_w_out, gdn_w_in, gdn_conv_w,
           gdn_a_log, gdn_dt_bias, gdn_onorm, gdn_w_out, sb_w_in, sb_w_out):
    bp, seq, d = x_prompt.shape
    bs, dec = x_sample.shape[:2]
    assert dec == CHUNK and seq % BAND_WINDOW == 0 and d == D_MODEL
    n_p = bp * seq
    n_s = bs * dec
    nc_p = seq // CHUNK
    off_s = n_p // CHUNK
    x = jnp.concatenate([x_prompt.reshape(n_p, d), x_sample.reshape(n_s, d)], axis=0)
    depth = ffn1_norm.shape[0]
    outs = {}
    for i in range(depth):
        m, j = i % 4, i // 4
        x = _ffn(x, ffn1_norm[i], ffn1_w_gu[i].astype(BF16), ffn1_w_down[i].astype(BF16))
        if m == 0:
            n_in = 2 * GLA_HEADS * GLA_DK + 2 * GLA_HEADS * GLA_DV + LANES
            (proj,) = _proj(x, mix_norm[i], _pad_cols(gla_w_in[j], n_in).astype(BF16), [(n_in, F32)],
                            _col_segs(n_in, lambda c: [(0, c, 1.0)]))
            wg2 = jnp.pad(gla_w_gate2[j], ((0, LANES - GLA_RANK), (0, 0))).astype(BF16)
            common = (wg2, gla_b_gate[j].reshape(1, -1), gla_onorm[j].reshape(1, -1))
            o_p, s_p = _gla(proj, *common, None, bp, nc_p, 0)
            o_s, s_s = _gla(proj, *common, state_gla[j], bs, 1, off_s)
            outs.setdefault("gla", []).append((s_p, s_s))
            w_out = gla_w_out[j]
        elif m == 1:
            band_proj = functools.partial(
                _proj, x, mix_norm[i], band_w_in[j].astype(BF16),
                [(d, BF16), (d, F32), (d, BF16), (d, F32), (d, BF16)],
                [(0, d, 0, [(0, 0, 1.0)]), (d, 2 * d, 1, [(1, 0, 1.0), (2, 0, 1.0)]),
                 (2 * d, 3 * d, None, [(3, 0, 1.0), (4, 0, 1.0)])],
                gains=(band_q_norm[j], band_k_norm[j]))
            qn_p, kn_p, knb_p, v_p, vb_p = band_proj(rows=(0, n_p))
            qn_s, kn_s, knb_s, v_s, vb_s = band_proj(rows=(n_p, n_p + n_s))
            bias = _band_bias(band_rel_bias[j])
            o_p = _band_attn_prompt(qn_p, knb_p, vb_p, bias, bp, seq)
            o_s = _band_attn_sample(qn_s, knb_s, vb_s, cache_band_k[j].reshape(bs, BAND_WINDOW, d),
                                    cache_band_v[j].reshape(bs, BAND_WINDOW, d), bias, bs, 0)
            keep = min(BAND_WINDOW, seq)
            shp = (BAND_HEADS, BAND_DH)
            tail = lambda a: jnp.stack([a[(b + 1) * seq - keep:(b + 1) * seq] for b in range(bp)]).reshape(bp, keep, *shp)
            outs.setdefault("band", []).append((
                tail(kn_p), tail(v_p), kn_s.reshape(bs, dec, *shp), v_s.reshape(bs, dec, *shp)))
            w_out = band_w_out[j]
        elif m == 2:
            n_in = GDN_CONV_CH + GDN_HEADS * GDN_DV + LANES
            (proj,) = _proj(x, mix_norm[i], _pad_cols(gdn_w_in[j], n_in).astype(BF16), [(n_in, F32)],
                            _col_segs(n_in, lambda c: [(0, c, 1.0)]))
            par = jnp.zeros((2, LANES), F32)
            par = par.at[0, GDN_HEADS:2 * GDN_HEADS].set(-jnp.exp(gdn_a_log[j]))
            par = par.at[1, GDN_HEADS:2 * GDN_HEADS].set(gdn_dt_bias[j])
            c0 = jnp.pad(state_gdn_conv[j], ((0, 0), (SUBLANES - (CONV_W - 1), 0), (0, 0)))
            pre_p = _gdn_prep(proj, gdn_conv_w[j], par, None, bp, seq, 0, ROW_TILE)
            pre_s = _gdn_prep(proj, gdn_conv_w[j], par, c0, bs, dec, n_p, dec)
            onorm = gdn_onorm[j].reshape(1, -1)
            o_p, s_p = _gdn(*pre_p, proj, onorm, None, bp, nc_p, 0)
            o_s, s_s = _gdn(*pre_s, proj, onorm, state_gdn[j], bs, 1, off_s)
            last = CONV_W - 1
            conv_p = jnp.stack([proj[(b + 1) * seq - last:(b + 1) * seq, :GDN_CONV_CH] for b in range(bp)])
            conv_s = proj[n_p:, :GDN_CONV_CH].reshape(bs, dec, -1)[:, dec - last:]
            outs.setdefault("gdn", []).append((s_p, conv_p, s_s, conv_s))
            w_out = gdn_w_out[j]
        else:
            sb_proj = functools.partial(
                _proj, x, mix_norm[i], sb_w_in[j].astype(BF16),
                [(d, BF16), (d, F32), (d, BF16), (d, F32), (d, BF16)],
                [(0, d, None, [(0, 0, -LOG2E * SB_DH ** -0.5)]), (d, 2 * d, None, [(1, 0, 1.0), (2, 0, 1.0)]),
                 (2 * d, 3 * d, None, [(3, 0, 1.0), (4, 0, 1.0)])])
            qb_p, k_p, kb_p, v_p, vb_p = sb_proj(rows=(0, n_p))
            qb_s, k_s, kb_s, v_s, vb_s = sb_proj(rows=(n_p, n_p + n_s))
            o_p = _sb_prompt(qb_p, kb_p, vb_p, bp, seq)
            o_s = _sb_sample(qb_s, kb_s, vb_s, cache_sb_k[j].reshape(bs, -1, d),
                             cache_sb_v[j].reshape(bs, -1, d), bs, 0)
            shp = (SB_HEADS, SB_DH)
            outs.setdefault("sb", []).append((
                k_p.reshape(bp, seq, *shp), v_p.reshape(bp, seq, *shp),
                k_s.reshape(bs, dec, *shp), v_s.reshape(bs, dec, *shp)))
            w_out = sb_w_out[j]
        x = _ffn(x, ffn2_norm[i], ffn2_w_gu[i].astype(BF16), ffn2_w_down[i].astype(BF16),
                 mix=(o_p, o_s, w_out.astype(BF16)))
    stack = lambda key, idx: jnp.stack([t[idx] for t in outs[key]])
    return (x[:n_p].reshape(bp, seq, d), x[n_p:].reshape(bs, dec, d),
            stack("gla", 0), stack("gla", 1),
            stack("band", 0), stack("band", 1), stack("band", 2), stack("band", 3),
            stack("gdn", 0), stack("gdn", 1), stack("gdn", 2), stack("gdn", 3),
            stack("sb", 0), stack("sb", 1), stack("sb", 2), stack("sb", 3))
```

```python
import functools

import jax
import jax.numpy as jnp
import numpy as np
from jax import lax
from jax.experimental import pallas as pl
from jax.experimental.pallas import tpu as pltpu

F32 = jnp.float32
BF16 = jnp.bfloat16

D_MODEL = 1024
CHUNK = 64
EPS = 1e-6
D_FF = 2816
GLA_HEADS, GLA_DK, GLA_DV, GLA_RANK, GLA_TAU = 4, 128, 256, 16, 16.0
BAND_HEADS, BAND_DH, LEFT_CHUNKS, REL_CLIP = 16, 64, 8, 256
BAND_WINDOW = LEFT_CHUNKS * CHUNK
BAND_SPAN = BAND_WINDOW + CHUNK
GDN_HEADS, GDN_DK, GDN_DV, CONV_W = 8, 128, 128, 4
GDN_CONV_CH = 2 * GDN_HEADS * GDN_DK + GDN_HEADS * GDN_DV
SB_HEADS, SB_DH = 16, 64
LOG2E = 1.4426950408889634

LANES = 128
SUBLANES = 8
VMEM_LIMIT = 56 * 1024 * 1024
ROW_TILE = 512
SB_TK = 256


def _params(*sem):
    return pltpu.CompilerParams(dimension_semantics=sem, vmem_limit_bytes=VMEM_LIMIT)


def _dot(a, b):
    return jnp.dot(a, b, preferred_element_type=F32)


def _dot_nt(a, b):
    return lax.dot_general(a, b, (((1,), (1,)), ((), ())), preferred_element_type=F32)


def _dot_tn(a, b):
    return lax.dot_general(a, b, (((0,), (0,)), ((), ())), preferred_element_type=F32)


def _split3(x):
    h1 = x.astype(BF16)
    r1 = x - h1.astype(F32)
    h2 = r1.astype(BF16)
    h3 = (r1 - h2.astype(F32)).astype(BF16)
    return h1, h2, h3


def _tri_dot(tri, x):
    h1, h2, h3 = _split3(x)
    return _dot(tri, h1) + _dot(tri, h2) + _dot(tri, h3)


def _dot_f32(a, b):
    a1 = a.astype(BF16)
    a2 = (a - a1.astype(F32)).astype(BF16)
    b1 = b.astype(BF16)
    b2 = (b - b1.astype(F32)).astype(BF16)
    return _dot(a1, b1) + (_dot(a1, b2) + _dot(a2, b1))


def _sigmoid(x):
    return 1.0 / (1.0 + jnp.exp(-x))


def _silu(x):
    return x * _sigmoid(x)


def _softplus(x):
    return jnp.maximum(x, 0.0) + jnp.log1p(jnp.exp(-jnp.abs(x)))


def _log_sigmoid(x):
    return -_softplus(-x)


def _rms(x, g):
    return x * lax.rsqrt(jnp.mean(x * x, axis=-1, keepdims=True) + EPS) * g


def _iota2(shape, dim):
    return lax.broadcasted_iota(jnp.int32, shape, dim)


def _ffn_kernel(*refs, fused, prompt_steps):
    if fused:
        x_ref, ap_ref, as_ref, wo_ref, g_ref, wg_ref, wu_ref, wd_ref, o_ref = refs
        a = jnp.where(pl.program_id(0) < prompt_steps, ap_ref[...], as_ref[...])
        x = x_ref[...] + _dot(a, wo_ref[...])
    else:
        x_ref, g_ref, wg_ref, wu_ref, wd_ref, o_ref = refs
        x = x_ref[...]
    xn = _rms(x, g_ref[...]).astype(BF16)
    g = _dot(xn, wg_ref[...])
    u = _dot(xn, wu_ref[...])
    h = (_silu(g) * u).astype(BF16)
    o_ref[...] = x + 0.5 * _dot(h, wd_ref[...])


def _ffn(x, gain, w_gu, w_down, mix=None):
    t, d = x.shape
    once = dict(pipeline_mode=pl.Buffered(1))
    row = pl.BlockSpec((ROW_TILE, d), lambda i: (i, 0))
    in_specs = [row]
    args = [x]
    prompt_steps = 0
    if mix is not None:
        prompt_steps = mix[0].shape[0] // ROW_TILE
        in_specs += [pl.BlockSpec((ROW_TILE, d), lambda i: (jnp.minimum(i, prompt_steps - 1), 0)),
                     pl.BlockSpec((ROW_TILE, d), lambda i: (jnp.maximum(i - prompt_steps, 0), 0)),
                     pl.BlockSpec((d, d), lambda i: (0, 0), **once)]
        args += list(mix)
    in_specs += [
        pl.BlockSpec((1, d), lambda i: (0, 0)),
        pl.BlockSpec((d, D_FF), lambda i: (0, 0), **once),
        pl.BlockSpec((d, D_FF), lambda i: (0, 1), **once),
        pl.BlockSpec((D_FF, d), lambda i: (0, 0), **once),
    ]
    return pl.pallas_call(
        functools.partial(_ffn_kernel, fused=mix is not None, prompt_steps=prompt_steps),
        out_shape=jax.ShapeDtypeStruct((t, d), F32),
        grid=(t // ROW_TILE,),
        in_specs=in_specs,
        out_specs=row,
        compiler_params=_params("parallel"),
        name="ffn",
    )(*args, gain.reshape(1, d), w_gu, w_gu, w_down)


def _pair_head_norm(x, gain2):
    low = _iota2(x.shape, 1) < BAND_DH
    sq = x * x
    s_lo = jnp.sum(jnp.where(low, sq, 0.0), axis=-1, keepdims=True)
    s_hi = jnp.sum(jnp.where(low, 0.0, sq), axis=-1, keepdims=True)
    ms = jnp.where(low, s_lo, s_hi) * (1.0 / BAND_DH)
    return x * lax.rsqrt(ms + EPS) * gain2


def _proj_kernel(*refs, segs, n_gain):
    x_ref, g_ref, w_ref = refs[:3]
    gains = refs[3:3 + n_gain]
    outs = refs[3 + n_gain:]
    xn = _rms(x_ref[...], g_ref[...]).astype(BF16)
    for c0, c1, gain_idx, dsts in segs:
        y = _dot(xn, w_ref[:, c0:c1])
        if gain_idx is not None:
            y = jnp.concatenate([_pair_head_norm(y[:, s:s + LANES], gains[gain_idx][...])
                                 for s in range(0, c1 - c0, LANES)], axis=1)
        for out_idx, off, scale in dsts:
            o_ref = outs[out_idx]
            o_ref[:, off:off + c1 - c0] = (y if scale == 1.0 else y * scale).astype(o_ref.dtype)


def _proj(x, gain, w, out_defs, segs, gains=(), rows=None):
    d = x.shape[1]
    r0, r1 = rows or (0, x.shape[0])
    first = r0 // ROW_TILE
    const = lambda shape: pl.BlockSpec(shape, lambda i: (0, 0))
    return pl.pallas_call(
        functools.partial(_proj_kernel, segs=tuple(segs), n_gain=len(gains)),
        out_shape=tuple(jax.ShapeDtypeStruct((r1 - r0, width), dt) for width, dt in out_defs),
        grid=((r1 - r0) // ROW_TILE,),
        in_specs=[pl.BlockSpec((ROW_TILE, d), lambda i: (first + i, 0)), const((1, d)),
                  pl.BlockSpec(w.shape, lambda i: (0, 0), pipeline_mode=pl.Buffered(1))]
        + [const((1, LANES))] * len(gains),
        out_specs=tuple(pl.BlockSpec((ROW_TILE, width), lambda i: (i, 0)) for width, _ in out_defs),
        compiler_params=_params("parallel"),
        name="proj",
    )(x, gain.reshape(1, d), w, *[jnp.tile(g, 2).reshape(1, LANES) for g in gains])


def _col_segs(n, dsts_of):
    return [(c, min(c + D_MODEL, n), None, dsts_of(c)) for c in range(0, n, D_MODEL)]


def _gla_kernel(*refs, has_init):
    if has_init:
        (q_ref, k_ref, v_ref, r_ref, gl_ref, wg2_ref, bg_ref, on_ref, s0_ref,
         o_ref, sfin_ref, st_ref) = refs
    else:
        (q_ref, k_ref, v_ref, r_ref, gl_ref, wg2_ref, bg_ref, on_ref,
         o_ref, sfin_ref, st_ref) = refs
    c = pl.program_id(1)

    @pl.when(c == 0)
    def _():
        for h in range(GLA_HEADS):
            if has_init:
                st_ref[h] = s0_ref[0, h].T
            else:
                st_ref[h] = jnp.zeros((GLA_DV, GLA_DK), F32)

    incl = _iota2((CHUNK, CHUNK), 0) >= _iota2((CHUNK, CHUNK), 1)
    tri = jnp.where(incl, 1.0, 0.0).astype(BF16)
    x = _dot(gl_ref[...].astype(BF16), wg2_ref[...]) + bg_ref[...]
    log_a = _log_sigmoid(x) * (1.0 / GLA_TAU)
    b = _tri_dot(tri, log_a)
    b_last = b[CHUNK - 1:CHUNK, :]
    q = q_ref[...] * (GLA_DK ** -0.5)
    k = k_ref[...]
    q_e = (q * jnp.exp(b)).astype(BF16)
    k_e = (k * jnp.exp(-b)).astype(BF16)
    k_l = (k * jnp.exp(b_last - b)).astype(BF16)
    d_last = jnp.exp(b_last)
    gain = on_ref[...]
    for h in range(GLA_HEADS):
        ks = slice(h * GLA_DK, (h + 1) * GLA_DK)
        vs = slice(h * GLA_DV, (h + 1) * GLA_DV)
        v = v_ref[:, vs].astype(BF16)
        st = st_ref[h]
        att = jnp.where(incl, _dot_nt(q_e[:, ks], k_e[:, ks]), 0.0)
        o = _dot(att.astype(BF16), v) + _dot_nt(q_e[:, ks], st.astype(BF16))
        st_ref[h] = st * d_last[:, ks] + _dot_tn(v, k_l[:, ks])
        o = _rms(o, gain) * _silu(r_ref[:, vs])
        o_ref[:, vs] = o.astype(BF16)

    @pl.when(c == pl.num_programs(1) - 1)
    def _():
        for h in range(GLA_HEADS):
            sfin_ref[0, h] = st_ref[h].T


def _gla(proj, w_gate2, b_gate, onorm, s0, nb, nc, row_off):
    qk = GLA_HEADS * GLA_DK
    vv = GLA_HEADS * GLA_DV
    row = lambda b, c: row_off + b * nc + c
    in_specs = [
        pl.BlockSpec((CHUNK, qk), lambda b, c: (row(b, c), 0)),
        pl.BlockSpec((CHUNK, qk), lambda b, c: (row(b, c), 1)),
        pl.BlockSpec((CHUNK, vv), lambda b, c: (row(b, c), 1)),
        pl.BlockSpec((CHUNK, vv), lambda b, c: (row(b, c), 2)),
        pl.BlockSpec((CHUNK, LANES), lambda b, c: (row(b, c), (2 * qk + 2 * vv) // LANES)),
        pl.BlockSpec((LANES, qk), lambda b, c: (0, 0)),
        pl.BlockSpec((1, qk), lambda b, c: (0, 0)),
        pl.BlockSpec((1, GLA_DV), lambda b, c: (0, 0)),
    ]
    args = [proj, proj, proj, proj, proj, w_gate2, b_gate, onorm]
    if s0 is not None:
        in_specs.append(pl.BlockSpec((1, GLA_HEADS, GLA_DK, GLA_DV), lambda b, c: (b, 0, 0, 0)))
        args.append(s0)
    return pl.pallas_call(
        functools.partial(_gla_kernel, has_init=s0 is not None),
        out_shape=(jax.ShapeDtypeStruct((nb * nc * CHUNK, vv), BF16),
                   jax.ShapeDtypeStruct((nb, GLA_HEADS, GLA_DK, GLA_DV), F32)),
        grid=(nb, nc),
        in_specs=in_specs,
        out_specs=(pl.BlockSpec((CHUNK, vv), lambda b, c: (b * nc + c, 0)),
                   pl.BlockSpec((1, GLA_HEADS, GLA_DK, GLA_DV), lambda b, c: (b, 0, 0, 0))),
        scratch_shapes=[pltpu.VMEM((GLA_HEADS, GLA_DV, GLA_DK), F32)],
        compiler_params=_params("parallel", "arbitrary"),
        name="gla",
    )(*args)


BIAS_WIDTH = 640


def _band_bias_kernel(w_ref, o_ref):
    for h in range(BAND_HEADS):
        x = jnp.broadcast_to(w_ref[h:h + 1, :], (CHUNK, BIAS_WIDTH))
        rolled = pltpu.roll(x, 0, 1, stride=1, stride_axis=0)
        o_ref[h // 2, (h % 2) * CHUNK:(h % 2 + 1) * CHUNK, :] = rolled[:, 0:BAND_SPAN]


def _band_bias(rel_bias):
    c = jnp.arange(BIAS_WIDTH)
    n = jnp.where(c <= BAND_SPAN, c, c - BIAS_WIDTH)
    w = rel_bias[:, jnp.clip(BAND_WINDOW - n, -REL_CLIP, REL_CLIP) + REL_CLIP]
    return pl.pallas_call(
        _band_bias_kernel,
        out_shape=jax.ShapeDtypeStruct((BAND_HEADS // 2, 2 * CHUNK, BAND_SPAN), F32),
        name="band_bias",
    )(w)


def _band_attn_kernel(q_ref, kp_ref, kc_ref, vp_ref, vc_ref, bias_ref, o_ref, kall_ref, vall_ref,
                      *, nq, prompt):
    if prompt:
        kall_ref[0:BAND_WINDOW] = kp_ref[...]
        vall_ref[0:BAND_WINDOW] = vp_ref[...]
    else:
        kall_ref[0:BAND_WINDOW] = kp_ref[0].astype(BF16)
        vall_ref[0:BAND_WINDOW] = vp_ref[0].astype(BF16)
    kall_ref[BAND_WINDOW:] = kc_ref[...]
    vall_ref[BAND_WINDOW:] = vc_ref[...]
    first_pos = (pl.program_id(1) - 1) * BAND_WINDOW if prompt else 0
    key_idx = _iota2((2 * CHUNK, BAND_SPAN), 1)
    pairs = range(D_MODEL // LANES)
    lanes = [slice(p * LANES, (p + 1) * LANES) for p in pairs]

    def chunk(t, carry):
        r0 = pl.multiple_of(t * CHUNK, CHUNK)
        qs = [_stack_pair(q_ref[pl.ds(r0, CHUNK), ls]) for ls in lanes]
        s = [_dot_nt(qs[p], kall_ref[pl.ds(r0, BAND_SPAN), lanes[p]]) * (BAND_DH ** -0.5) + bias_ref[p]
             for p in pairs]
        if prompt:
            visible = first_pos + r0 + key_idx >= 0
            s = [jnp.where(visible, x, -jnp.inf) for x in s]
        e = [jnp.exp(x - jnp.max(x, axis=-1, keepdims=True)) for x in s]
        prob = [(x / jnp.sum(x, axis=-1, keepdims=True)).astype(BF16) for x in e]
        pv = [_dot(prob[p], vall_ref[pl.ds(r0, BAND_SPAN), lanes[p]]) for p in pairs]
        for p in pairs:
            o_ref[pl.ds(r0, CHUNK), lanes[p]] = _unstack_pair(pv[p]).astype(BF16)
        return carry

    lax.fori_loop(0, nq, chunk, 0)


def _band_attn_prompt(qn, knb, vb, bias, nb, seq):
    d = D_MODEL
    blk = BAND_WINDOW
    nblk = seq // blk
    cur = pl.BlockSpec((blk, d), lambda b, i: (b * nblk + i, 0))
    prev = pl.BlockSpec((blk, d), lambda b, i: (b * nblk + jnp.maximum(i - 1, 0), 0))
    return pl.pallas_call(
        functools.partial(_band_attn_kernel, nq=blk // CHUNK, prompt=True),
        out_shape=jax.ShapeDtypeStruct((nb * seq, d), BF16),
        grid=(nb, nblk),
        in_specs=[cur, prev, cur, prev, cur,
                  pl.BlockSpec(bias.shape, lambda b, i: (0, 0, 0))],
        out_specs=cur,
        scratch_shapes=[pltpu.VMEM((2 * blk, d), BF16), pltpu.VMEM((2 * blk, d), BF16)],
        compiler_params=_params("parallel", "arbitrary"),
        name="band_attn_prompt",
    )(qn, knb, knb, vb, vb, bias)


def _band_attn_sample(qn, knb, vb, cache_k, cache_v, bias, nb, row_off):
    d = D_MODEL
    new = pl.BlockSpec((CHUNK, d), lambda b: (row_off + b, 0))
    old = pl.BlockSpec((1, BAND_WINDOW, d), lambda b: (b, 0, 0))
    return pl.pallas_call(
        functools.partial(_band_attn_kernel, nq=1, prompt=False),
        out_shape=jax.ShapeDtypeStruct((nb * CHUNK, d), BF16),
        grid=(nb,),
        in_specs=[new, old, new, old, new, pl.BlockSpec(bias.shape, lambda b: (0, 0, 0))],
        out_specs=pl.BlockSpec((CHUNK, d), lambda b: (b, 0)),
        scratch_shapes=[pltpu.VMEM((BAND_SPAN, d), BF16), pltpu.VMEM((BAND_SPAN, d), BF16)],
        compiler_params=_params("parallel"),
        name="band_attn_sample",
    )(qn, cache_k, knb, cache_v, vb, bias)


def _gdn_prep_kernel(*refs, has_init):
    if has_init:
        x_ref, raw_ref, cw_ref, par_ref, c0_ref, q_ref, k_ref, v_ref, bg_ref, carry_ref = refs
    else:
        x_ref, raw_ref, cw_ref, par_ref, q_ref, k_ref, v_ref, bg_ref, carry_ref = refs
    tb = x_ref.shape[0]

    @pl.when(pl.program_id(1) == 0)
    def _():
        if has_init:
            carry_ref[...] = c0_ref[0]
        else:
            carry_ref[...] = jnp.zeros_like(carry_ref)

    x = x_ref[...]
    w = cw_ref[...]
    x8 = x[0:SUBLANES]
    c8 = carry_ref[...]
    row8 = _iota2(x8.shape, 0)
    y = x * w[CONV_W - 1:CONV_W]
    y8 = x8 * w[CONV_W - 1:CONV_W]
    for s in range(1, CONV_W):
        ws = w[CONV_W - 1 - s:CONV_W - s]
        y = y + pltpu.roll(x, s, 0) * ws
        y8 = y8 + jnp.where(row8 < s, pltpu.roll(c8, s, 0), pltpu.roll(x8, s, 0)) * ws
    carry_ref[...] = x[tb - SUBLANES:tb]
    if tb > SUBLANES:
        y = jnp.concatenate([y8, y[SUBLANES:]], axis=0)
    else:
        y = y8
    y = _silu(y)
    qk = GDN_HEADS * GDN_DK
    for h in range(GDN_HEADS):
        qh = y[:, h * GDN_DK:(h + 1) * GDN_DK]
        kh = y[:, qk + h * GDN_DK:qk + (h + 1) * GDN_DK]
        q_ref[h] = qh * lax.rsqrt(jnp.sum(qh * qh, axis=-1, keepdims=True) + EPS)
        k_ref[h] = kh * lax.rsqrt(jnp.sum(kh * kh, axis=-1, keepdims=True) + EPS)
        v_ref[h] = y[:, 2 * qk + h * GDN_DV:2 * qk + (h + 1) * GDN_DV]
    raw = raw_ref[...]
    beta = _sigmoid(raw)
    g = par_ref[0:1, :] * _softplus(raw + par_ref[1:2, :])
    bg_ref[...] = jnp.where(_iota2(raw.shape, 1) < GDN_HEADS, beta, g)


def _gdn_prep(proj, conv_w, par, c0, nb, seq, row_off, tb):
    nblk = seq // tb
    base = row_off // tb
    row = lambda b, i: base + b * nblk + i
    in_specs = [
        pl.BlockSpec((tb, GDN_CONV_CH), lambda b, i: (row(b, i), 0)),
        pl.BlockSpec((tb, LANES), lambda b, i: (row(b, i), (GDN_CONV_CH + GDN_HEADS * GDN_DV) // LANES)),
        pl.BlockSpec((CONV_W, GDN_CONV_CH), lambda b, i: (0, 0)),
        pl.BlockSpec((2, LANES), lambda b, i: (0, 0)),
    ]
    args = [proj, proj, conv_w, par]
    if c0 is not None:
        in_specs.append(pl.BlockSpec((1, SUBLANES, GDN_CONV_CH), lambda b, i: (b, 0, 0)))
        args.append(c0)
    n = nb * seq
    head = pl.BlockSpec((GDN_HEADS, tb, GDN_DK), lambda b, i: (0, b * nblk + i, 0))
    return pl.pallas_call(
        functools.partial(_gdn_prep_kernel, has_init=c0 is not None),
        out_shape=(jax.ShapeDtypeStruct((GDN_HEADS, n, GDN_DK), F32),
                   jax.ShapeDtypeStruct((GDN_HEADS, n, GDN_DK), F32),
                   jax.ShapeDtypeStruct((GDN_HEADS, n, GDN_DV), F32),
                   jax.ShapeDtypeStruct((n, LANES), F32)),
        grid=(nb, nblk),
        in_specs=in_specs,
        out_specs=(head, head, head, pl.BlockSpec((tb, LANES), lambda b, i: (b * nblk + i, 0))),
        scratch_shapes=[pltpu.VMEM((SUBLANES, GDN_CONV_CH), F32)],
        compiler_params=_params("parallel", "arbitrary"),
        name="gdn_prep",
    )(*args)


def _gdn_kernel(*refs, has_init):
    if has_init:
        q_ref, k_ref, v_ref, bg_ref, gate_ref, on_ref, s0_ref, o_ref, sfin_ref, s_ref = refs
    else:
        q_ref, k_ref, v_ref, bg_ref, gate_ref, on_ref, o_ref, sfin_ref, s_ref = refs
    c = pl.program_id(1)

    @pl.when(c == 0)
    def _():
        if has_init:
            s_ref[...] = s0_ref[0]
        else:
            s_ref[...] = jnp.zeros_like(s_ref)

    ri = _iota2((CHUNK, CHUNK), 0)
    ci = _iota2((CHUNK, CHUNK), 1)
    incl = ri >= ci
    strict = ri > ci
    tri = jnp.where(incl, 1.0, 0.0).astype(BF16)
    eye = jnp.where(ri == ci, 1.0, 0.0)
    bg = bg_ref[...]
    cs = _tri_dot(tri, bg)
    cs_t = cs.T
    gain = on_ref[...]
    heads = range(GDN_HEADS)
    beta = [bg[:, h:h + 1] for h in heads]
    gcol = [cs[:, GDN_HEADS + h:GDN_HEADS + h + 1] for h in heads]
    glast = [g[CHUNK - 1:CHUNK, :] for g in gcol]
    gam = [jnp.exp(jnp.where(incl, gcol[h] - cs_t[GDN_HEADS + h:GDN_HEADS + h + 1, :], -jnp.inf)) for h in heads]
    egc = [jnp.exp(g) for g in gcol]
    k = [k_ref[h] for h in heads]
    k16 = [x.astype(BF16) for x in k]
    kb = [k[h] * beta[h] for h in heads]
    m = [jnp.where(strict, _dot_nt(kb[h].astype(BF16), k16[h]) * gam[h], 0.0) for h in heads]
    d = [eye - jnp.where((ri == ci + 1) & ((ci & 1) == 0), m[h], 0.0) for h in heads]
    for lvl in range(1, 6):
        sub = (((ri >> lvl) & 1) == 1) & (((ci >> lvl) & 1) == 0) & ((ri >> (lvl + 1)) == (ci >> (lvl + 1)))
        y = [_dot_f32(d[h], jnp.where(sub, m[h], 0.0)) for h in heads]
        x = [_dot_f32(y[h], d[h]) for h in heads]
        d = [d[h] - x[h] for h in heads]
    t16 = [t.astype(BF16) for t in d]
    u = [_dot(t16[h], (v_ref[h] * beta[h]).astype(BF16)) for h in heads]
    w = [_dot(t16[h], (kb[h] * egc[h]).astype(BF16)) for h in heads]
    q = [q_ref[h] * (GDN_DK ** -0.5) for h in heads]
    att = [_dot_nt(q[h].astype(BF16), k16[h]) * gam[h] for h in heads]
    q_e = [(q[h] * egc[h]).astype(BF16) for h in heads]
    k_l = [(k[h] * jnp.exp(glast[h] - gcol[h])).astype(BF16) for h in heads]
    s = [s_ref[h] for h in heads]
    s16 = [x.astype(BF16) for x in s]
    v_new = [u[h] - _dot(w[h].astype(BF16), s16[h]) for h in heads]
    v16 = [x.astype(BF16) for x in v_new]
    o = [_dot(q_e[h], s16[h]) + _dot(att[h].astype(BF16), v16[h]) for h in heads]
    for h in heads:
        s_ref[h] = s[h] * jnp.exp(glast[h]) + _dot_tn(k_l[h], v16[h])
        vs = slice(h * GDN_DV, (h + 1) * GDN_DV)
        o_ref[:, vs] = (_rms(o[h], gain) * _silu(gate_ref[:, vs])).astype(BF16)

    @pl.when(c == pl.num_programs(1) - 1)
    def _():
        sfin_ref[0] = s_ref[...]


def _gdn(qh, kh, vh, bg, proj, onorm, s0, nb, nc, row_off):
    vv = GDN_HEADS * GDN_DV
    head = pl.BlockSpec((GDN_HEADS, CHUNK, GDN_DK), lambda b, c: (0, b * nc + c, 0))
    in_specs = [
        head, head, head,
        pl.BlockSpec((CHUNK, LANES), lambda b, c: (b * nc + c, 0)),
        pl.BlockSpec((CHUNK, vv), lambda b, c: (row_off + b * nc + c, GDN_CONV_CH // vv)),
        pl.BlockSpec((1, GDN_DV), lambda b, c: (0, 0)),
    ]
    args = [qh, kh, vh, bg, proj, onorm]
    state = pl.BlockSpec((1, GDN_HEADS, GDN_DK, GDN_DV), lambda b, c: (b, 0, 0, 0))
    if s0 is not None:
        in_specs.append(state)
        args.append(s0)
    return pl.pallas_call(
        functools.partial(_gdn_kernel, has_init=s0 is not None),
        out_shape=(jax.ShapeDtypeStruct((nb * nc * CHUNK, vv), BF16),
                   jax.ShapeDtypeStruct((nb, GDN_HEADS, GDN_DK, GDN_DV), F32)),
        grid=(nb, nc),
        in_specs=in_specs,
        out_specs=(pl.BlockSpec((CHUNK, vv), lambda b, c: (b * nc + c, 0)), state),
        scratch_shapes=[pltpu.VMEM((GDN_HEADS, GDN_DK, GDN_DV), F32)],
        compiler_params=_params("parallel", "arbitrary"),
        name="gdn",
    )(*args)


def _neg_abs(x):
    bits = lax.bitcast_convert_type(x, jnp.uint32) | jnp.uint32(0x80000000)
    return lax.bitcast_convert_type(bits, F32)


def _stack_pair(q2):
    low = _iota2(q2.shape, 1) < SB_DH
    return jnp.concatenate([jnp.where(low, q2, 0), jnp.where(low, 0, q2)], axis=0)


def _unstack_pair(acc):
    tq = acc.shape[0] // 2
    low = _iota2((tq, LANES), 1) < SB_DH
    return jnp.where(low, acc[:tq], acc[tq:])


def _upper(tk):
    return jnp.where(_iota2((tk, tk), 0) > _iota2((tk, tk), 1), 1.0, 0.0).astype(BF16)


def _diag_mask(tq, tk):
    r = _iota2((2 * tq, tk), 0)
    r = jnp.where(r >= tq, r - tq, r)
    return _iota2((2 * tq, tk), 1) < r


SB_NEUTRAL = 1e30
SB_LAG = 4
SB_CHAINS = 1


def _sb_log_terms(zn, mask):
    l1m = jnp.minimum(zn, 0.0) - jnp.log2(1.0 + jnp.exp2(_neg_abs(zn)))
    log_b = l1m - zn
    if mask is not None:
        l1m = jnp.where(mask, l1m, 0.0)
        log_b = jnp.where(mask, log_b, -SB_NEUTRAL)
    return l1m, log_b


def _sb_direct(qs, k2, v2, mask):
    l1m, log_b = _sb_log_terms(_dot_nt(qs, k2), mask)
    a = jnp.exp2(log_b + _dot(l1m.astype(BF16), _upper(k2.shape[0])))
    return jnp.sum(l1m, axis=-1, keepdims=True), _dot(a.astype(BF16), v2)


def _sb_pipeline(nhalf, nchain, q_of, k_of, v_of, st_of, upper, scratch, mask):
    z_ref, hi_ref, logb_ref, tot_ref, cs_ref, a_ref, run_ref, acc_ref = scratch
    logb_ref[...] = jnp.full(logb_ref.shape, -SB_NEUTRAL, F32)
    tot_ref[...] = jnp.zeros(tot_ref.shape, F32)
    for c in range(nchain):
        z_ref[2 * c + 1] = jnp.full(z_ref.shape[1:], SB_NEUTRAL, F32)
        hi_ref[2 * c + 1] = jnp.zeros(hi_ref.shape[1:], BF16)
        cs_ref[2 * c + 1] = jnp.zeros(cs_ref.shape[1:], F32)
        a_ref[2 * c + 1] = jnp.zeros(a_ref.shape[1:], BF16)

    def half(h, p):
        for c in range(nchain):
            rows = st_of(h - 4, c)
            acc_ref[rows, :] += _dot(a_ref[2 * c + 1 - p], v_of(h - 4, c))
        for c in range(nchain):
            rows = st_of(h - 3, c)
            run = run_ref[rows, :]
            a_ref[2 * c + p] = jnp.exp2(cs_ref[2 * c + 1 - p] + (logb_ref[2 * c + p] + run)).astype(BF16)
            run_ref[rows, :] = run + tot_ref[2 * c + p]
        for c in range(nchain):
            cs_ref[2 * c + p] = _dot(hi_ref[2 * c + 1 - p], upper)
        for c in range(nchain):
            l1m, logb_ref[2 * c + p] = _sb_log_terms(z_ref[2 * c + 1 - p], mask)
            hi_ref[2 * c + p] = l1m.astype(BF16)
            tot_ref[2 * c + p] = jnp.sum(l1m, axis=-1, keepdims=True)
        for c in range(nchain):
            z_ref[2 * c + p] = _dot_nt(q_of(h, c), k_of(h, c))

    def body(i, _):
        half(2 * i, 0)
        half(2 * i + 1, 1)
        return 0

    assert nhalf % 2 == 0
    lax.fori_loop(0, nhalf // 2, body, 0)


def _sb_rings(nchain, rows, tk):
    slots = 2 * nchain
    return [pltpu.VMEM((slots, rows, tk), F32), pltpu.VMEM((slots, rows, tk), BF16),
            pltpu.VMEM((slots, rows, tk), F32), pltpu.VMEM((slots, rows, 1), F32),
            pltpu.VMEM((slots, rows, tk), F32), pltpu.VMEM((slots, rows, tk), BF16)]


def _sb_prompt_items(nq):
    diag = [(i, i) for i in range(nq)]
    off = [(j, i) for j in range(nq - 2, -1, -1) for i in range(j + 1, nq)]
    pad = [(0, nq)] * (SB_CHAINS * SB_LAG)
    assert len(diag) % (2 * SB_CHAINS) == 0 and len(off) % (2 * SB_CHAINS) == 0
    rows = []
    for items in (diag, off):
        padded = pad + items + pad
        rows += [[j for j, _ in padded], [i for _, i in padded]]
    width = max(len(r) for r in rows)
    tab = np.zeros((4, width), np.int32)
    for r, row in enumerate(rows):
        tab[r, :len(row)] = row
    return jnp.asarray(tab), len(diag), len(off)


def _sb_prompt_kernel(tab_ref, q_ref, k_ref, v_ref, o_ref, qs_ref, run_ref, acc_ref, *rings, nq, ndiag, noff):
    tk = SB_TK
    rows = 2 * tk

    def stack(i, _):
        qs_ref[pl.ds(pl.multiple_of(i * rows, rows), rows), :] = _stack_pair(
            q_ref[pl.ds(pl.multiple_of(i * tk, tk), tk), :])
        return 0

    lax.fori_loop(0, nq, stack, 0)
    qs_ref[nq * rows:(nq + 1) * rows, :] = jnp.zeros((rows, LANES), BF16)
    run_ref[...] = jnp.zeros(run_ref.shape, F32)
    acc_ref[...] = jnp.zeros(acc_ref.shape, F32)
    upper = _upper(tk)
    scratch = list(rings) + [run_ref, acc_ref]
    for row, nitems, mask in ((0, ndiag, _diag_mask(tk, tk)), (2, noff, None)):
        col = lambda hh, c: SB_CHAINS * (hh + SB_LAG) + c
        q_rows = lambda hh, c: pl.ds(pl.multiple_of(tab_ref[row + 1, col(hh, c)] * rows, rows), rows)
        k_rows = lambda hh, c: pl.ds(pl.multiple_of(tab_ref[row, col(hh, c)] * tk, tk), tk)
        _sb_pipeline(nitems // SB_CHAINS + SB_LAG, SB_CHAINS,
                     lambda hh, c: qs_ref[q_rows(hh, c), :],
                     lambda hh, c: k_ref[k_rows(hh, c), :],
                     lambda hh, c: v_ref[k_rows(hh, c), :],
                     q_rows, upper, scratch, mask)

    def emit(i, _):
        o_ref[pl.ds(pl.multiple_of(i * tk, tk), tk), :] = _unstack_pair(
            acc_ref[pl.ds(pl.multiple_of(i * rows, rows), rows), :]).astype(BF16)
        return 0

    lax.fori_loop(0, nq, emit, 0)


def _sb_prompt(qb, kb, vb, nb, seq):
    d = D_MODEL
    npair = d // LANES
    nq = seq // SB_TK
    rows = 2 * SB_TK
    tab, ndiag, noff = _sb_prompt_items(nq)
    col = pl.BlockSpec((seq, LANES), lambda b, p, tab: (b, p))
    return pl.pallas_call(
        functools.partial(_sb_prompt_kernel, nq=nq, ndiag=ndiag, noff=noff),
        out_shape=jax.ShapeDtypeStruct((nb * seq, d), BF16),
        grid_spec=pltpu.PrefetchScalarGridSpec(
            num_scalar_prefetch=1,
            grid=(nb, npair),
            in_specs=[col, col, col],
            out_specs=col,
            scratch_shapes=[pltpu.VMEM(((nq + 1) * rows, LANES), BF16), pltpu.VMEM(((nq + 1) * rows, 1), F32),
                            pltpu.VMEM(((nq + 1) * rows, LANES), F32)] + _sb_rings(SB_CHAINS, rows, SB_TK)),
        compiler_params=_params("parallel", "arbitrary"),
        name="sb_prompt",
    )(tab, qb, kb, vb)


def _sb_sample_kernel(q_ref, kn_ref, vn_ref, kc_ref, vc_ref, o_ref, run_ref, acc_ref, *rings):
    tq, tk = CHUNK, SB_TK
    rows = 2 * tq
    nblk = kc_ref.shape[1] // tk
    lanes = lambda c: slice(c * LANES, (c + 1) * LANES)
    qs = [_stack_pair(q_ref[:, lanes(c)]) for c in range(2)]
    for c in range(2):
        run, acc = _sb_direct(qs[c], kn_ref[:, lanes(c)], vn_ref[:, lanes(c)], _diag_mask(tq, tq))
        run_ref[c * rows:(c + 1) * rows, :] = run
        acc_ref[c * rows:(c + 1) * rows, :] = acc

    def k_rows(hh):
        return pl.ds(pl.multiple_of(jnp.clip(nblk - 1 - hh, 0, nblk - 1) * tk, tk), tk)

    def st_of(hh, c):
        real = (hh >= 0) & (hh < nblk)
        return pl.ds(pl.multiple_of(jnp.where(real, c, 2) * rows, rows), rows)

    _sb_pipeline(nblk + SB_LAG, 2,
                 lambda hh, c: qs[c],
                 lambda hh, c: kc_ref[0, k_rows(hh), lanes(c)].astype(BF16),
                 lambda hh, c: vc_ref[0, k_rows(hh), lanes(c)].astype(BF16),
                 st_of, _upper(tk), list(rings) + [run_ref, acc_ref], None)
    for c in range(2):
        o_ref[:, lanes(c)] = _unstack_pair(acc_ref[c * rows:(c + 1) * rows, :]).astype(BF16)


def _sb_sample(qb, kb, vb, cache_k, cache_v, nb, row_off):
    d = D_MODEL
    wide = 2 * LANES
    ngrp = d // wide
    past = cache_k.shape[1]
    rows = 2 * CHUNK
    new = pl.BlockSpec((CHUNK, wide), lambda b, g: (row_off + b, g))
    old = pl.BlockSpec((1, past, wide), lambda b, g: (b, 0, g))
    return pl.pallas_call(
        _sb_sample_kernel,
        out_shape=jax.ShapeDtypeStruct((nb * CHUNK, d), BF16),
        grid=(nb, ngrp),
        in_specs=[new, new, new, old, old],
        out_specs=pl.BlockSpec((CHUNK, wide), lambda b, g: (b, g)),
        scratch_shapes=[pltpu.VMEM((3 * rows, 1), F32), pltpu.VMEM((3 * rows, LANES), F32)]
        + _sb_rings(2, rows, SB_TK),
        compiler_params=_params("parallel", "parallel"),
        name="sb_sample",
    )(qb, kb, vb, cache_k, cache_v)


def _pad_cols(w, n):
    return jnp.pad(w, ((0, 0), (0, n - w.shape[1])))


def kernel(x_prompt, x_sample, state_gla, cache_band_k, cache_band_v, state_gdn, state_gdn_conv,
           cache_sb_k, cache_sb_v, ffn1_norm, ffn1_w_gu, ffn1_w_down, mix_norm, ffn2_norm,
           ffn2_w_gu, ffn2_w_down, gla_w_in, gla_w_gate2, gla_b_gate, gla_onorm, gla_w_out,
           band_w_in, band_q_norm, band_k_norm, band_rel_bias, band_w_out, gdn_w_in, gdn_conv_w,
           gdn_a_log, gdn_dt_bias, gdn_onorm, gdn_w_out, sb_w_in, sb_w_out):
    bp, seq, d = x_prompt.shape
    bs, dec = x_sample.shape[:2]
    assert dec == CHUNK and seq % BAND_WINDOW == 0 and d == D_MODEL
    n_p = bp * seq
    n_s = bs * dec
    nc_p = seq // CHUNK
    off_s = n_p // CHUNK
    x = jnp.concatenate([x_prompt.reshape(n_p, d), x_sample.reshape(n_s, d)], axis=0)
    depth = ffn1_norm.shape[0]
    outs = {}
    for i in range(depth):
        m, j = i % 4, i // 4
        x = _ffn(x, ffn1_norm[i], ffn1_w_gu[i].astype(BF16), ffn1_w_down[i].astype(BF16))
        if m == 0:
            n_in = 2 * GLA_HEADS * GLA_DK + 2 * GLA_HEADS * GLA_DV + LANES
            (proj,) = _proj(x, mix_norm[i], _pad_cols(gla_w_in[j], n_in).astype(BF16), [(n_in, F32)],
                            _col_segs(n_in, lambda c: [(0, c, 1.0)]))
            wg2 = jnp.pad(gla_w_gate2[j], ((0, LANES - GLA_RANK), (0, 0))).astype(BF16)
            common = (wg2, gla_b_gate[j].reshape(1, -1), gla_onorm[j].reshape(1, -1))
            o_p, s_p = _gla(proj, *common, None, bp, nc_p, 0)
            o_s, s_s = _gla(proj, *common, state_gla[j], bs, 1, off_s)
            outs.setdefault("gla", []).append((s_p, s_s))
            w_out = gla_w_out[j]
        elif m == 1:
            band_proj = functools.partial(
                _proj, x, mix_norm[i], band_w_in[j].astype(BF16),
                [(d, BF16), (d, F32), (d, BF16), (d, F32), (d, BF16)],
                [(0, d, 0, [(0, 0, 1.0)]), (d, 2 * d, 1, [(1, 0, 1.0), (2, 0, 1.0)]),
                 (2 * d, 3 * d, None, [(3, 0, 1.0), (4, 0, 1.0)])],
                gains=(band_q_norm[j], band_k_norm[j]))
            qn_p, kn_p, knb_p, v_p, vb_p = band_proj(rows=(0, n_p))
            qn_s, kn_s, knb_s, v_s, vb_s = band_proj(rows=(n_p, n_p + n_s))
            bias = _band_bias(band_rel_bias[j])
            o_p = _band_attn_prompt(qn_p, knb_p, vb_p, bias, bp, seq)
            o_s = _band_attn_sample(qn_s, knb_s, vb_s, cache_band_k[j].reshape(bs, BAND_WINDOW, d),
                                    cache_band_v[j].reshape(bs, BAND_WINDOW, d), bias, bs, 0)
            keep = min(BAND_WINDOW, seq)
            shp = (BAND_HEADS, BAND_DH)
            tail = lambda a: jnp.stack([a[(b + 1) * seq - keep:(b + 1) * seq] for b in range(bp)]).reshape(bp, keep, *shp)
            outs.setdefault("band", []).append((
                tail(kn_p), tail(v_p), kn_s.reshape(bs, dec, *shp), v_s.reshape(bs, dec, *shp)))
            w_out = band_w_out[j]
        elif m == 2:
            n_in = GDN_CONV_CH + GDN_HEADS * GDN_DV + LANES
            (proj,) = _proj(x, mix_norm[i], _pad_cols(gdn_w_in[j], n_in).astype(BF16), [(n_in, F32)],
                            _col_segs(n_in, lambda c: [(0, c, 1.0)]))
            par = jnp.zeros((2, LANES), F32)
            par = par.at[0, GDN_HEADS:2 * GDN_HEADS].set(-jnp.exp(gdn_a_log[j]))
            par = par.at[1, GDN_HEADS:2 * GDN_HEADS].set(gdn_dt_bias[j])
            c0 = jnp.pad(state_gdn_conv[j], ((0, 0), (SUBLANES - (CONV_W - 1), 0), (0, 0)))
            pre_p = _gdn_prep(proj, gdn_conv_w[j], par, None, bp, seq, 0, ROW_TILE)
            pre_s = _gdn_prep(proj, gdn_conv_w[j], par, c0, bs, dec, n_p, dec)
            onorm = gdn_onorm[j].reshape(1, -1)
            o_p, s_p = _gdn(*pre_p, proj, onorm, None, bp, nc_p, 0)
            o_s, s_s = _gdn(*pre_s, proj, onorm, state_gdn[j], bs, 1, off_s)
            last = CONV_W - 1
            conv_p = jnp.stack([proj[(b + 1) * seq - last:(b + 1) * seq, :GDN_CONV_CH] for b in range(bp)])
            conv_s = proj[n_p:, :GDN_CONV_CH].reshape(bs, dec, -1)[:, dec - last:]
            outs.setdefault("gdn", []).append((s_p, conv_p, s_s, conv_s))
            w_out = gdn_w_out[j]
        else:
            sb_proj = functools.partial(
                _proj, x, mix_norm[i], sb_w_in[j].astype(BF16),
                [(d, BF16), (d, F32), (d, BF16), (d, F32), (d, BF16)],
                [(0, d, None, [(0, 0, -LOG2E * SB_DH ** -0.5)]), (d, 2 * d, None, [(1, 0, 1.0), (2, 0, 1.0)]),
                 (2 * d, 3 * d, None, [(3, 0, 1.0), (4, 0, 1.0)])])
            qb_p, k_p, kb_p, v_p, vb_p = sb_proj(rows=(0, n_p))
            qb_s, k_s, kb_s, v_s, vb_s = sb_proj(rows=(n_p, n_p + n_s))
            o_p = _sb_prompt(qb_p, kb_p, vb_p, bp, seq)
            o_s = _sb_sample(qb_s, kb_s, vb_s, cache_sb_k[j].reshape(bs, -1, d),
                             cache_sb_v[j].reshape(bs, -1, d), bs, 0)
            shp = (SB_HEADS, SB_DH)
            outs.setdefault("sb", []).append((
                k_p.reshape(bp, seq, *shp), v_p.reshape(bp, seq, *shp),
                k_s.reshape(bs, dec, *shp), v_s.reshape(bs, dec, *shp)))
            w_out = sb_w_out[j]
        x = _ffn(x, ffn2_norm[i], ffn2_w_gu[i].astype(BF16), ffn2_w_down[i].astype(BF16),
                 mix=(o_p, o_s, w_out.astype(BF16)))
    stack = lambda key, idx: jnp.stack([t[idx] for t in outs[key]])
    return (x[:n_p].reshape(bp, seq, d), x[n_p:].reshape(bs, dec, d),
            stack("gla", 0), stack("gla", 1),
            stack("band", 0), stack("band", 1), stack("band", 2), stack("band", 3),
            stack("gdn", 0), stack("gdn", 1), stack("gdn", 2), stack("gdn", 3),
            stack("sb", 0), stack("sb", 1), stack("sb", 2), stack("sb", 3))
```

```python
import functools

import jax
import jax.numpy as jnp
import numpy as np
from jax import lax
from jax.experimental import pallas as pl
from jax.experimental.pallas import tpu as pltpu

F32 = jnp.float32
BF16 = jnp.bfloat16

D_MODEL = 1024
CHUNK = 64
EPS = 1e-6
D_FF = 2816
GLA_HEADS, GLA_DK, GLA_DV, GLA_RANK, GLA_TAU = 4, 128, 256, 16, 16.0
BAND_HEADS, BAND_DH, LEFT_CHUNKS, REL_CLIP = 16, 64, 8, 256
BAND_WINDOW = LEFT_CHUNKS * CHUNK
BAND_SPAN = BAND_WINDOW + CHUNK
GDN_HEADS, GDN_DK, GDN_DV, CONV_W = 8, 128, 128, 4
GDN_CONV_CH = 2 * GDN_HEADS * GDN_DK + GDN_HEADS * GDN_DV
SB_HEADS, SB_DH = 16, 64
LOG2E = 1.4426950408889634

LANES = 128
SUBLANES = 8
VMEM_LIMIT = 56 * 1024 * 1024
ROW_TILE = 512
SB_TK = 256
GROUP = 2


def _params(*sem):
    return pltpu.CompilerParams(dimension_semantics=sem, vmem_limit_bytes=VMEM_LIMIT)


def _dot(a, b):
    return jnp.dot(a, b, preferred_element_type=F32)


def _dot_nt(a, b):
    return lax.dot_general(a, b, (((1,), (1,)), ((), ())), preferred_element_type=F32)


def _dot_tn(a, b):
    return lax.dot_general(a, b, (((0,), (0,)), ((), ())), preferred_element_type=F32)


def _split3(x):
    h1 = x.astype(BF16)
    r1 = x - h1.astype(F32)
    h2 = r1.astype(BF16)
    h3 = (r1 - h2.astype(F32)).astype(BF16)
    return h1, h2, h3


def _tri_dot(tri, x):
    h1, h2, h3 = _split3(x)
    return _dot(tri, h1) + _dot(tri, h2) + _dot(tri, h3)


def _dot_f32(a, b):
    a1 = a.astype(BF16)
    a2 = (a - a1.astype(F32)).astype(BF16)
    b1 = b.astype(BF16)
    b2 = (b - b1.astype(F32)).astype(BF16)
    return _dot(a1, b1) + (_dot(a1, b2) + _dot(a2, b1))


def _sigmoid(x):
    return 1.0 / (1.0 + jnp.exp(-x))


def _silu(x):
    return x * _sigmoid(x)


def _softplus(x):
    return jnp.maximum(x, 0.0) + jnp.log1p(jnp.exp(-jnp.abs(x)))


def _log_sigmoid(x):
    return -_softplus(-x)


def _rms(x, g):
    return x * lax.rsqrt(jnp.mean(x * x, axis=-1, keepdims=True) + EPS) * g


def _iota2(shape, dim):
    return lax.broadcasted_iota(jnp.int32, shape, dim)


def _ffn_kernel(*refs, fused, prompt_steps):
    if fused:
        x_ref, ap_ref, as_ref, wo_ref, g_ref, wg_ref, wu_ref, wd_ref, o_ref = refs
        a = jnp.where(pl.program_id(0) < prompt_steps, ap_ref[...], as_ref[...])
        x = x_ref[...] + _dot(a, wo_ref[...])
    else:
        x_ref, g_ref, wg_ref, wu_ref, wd_ref, o_ref = refs
        x = x_ref[...]
    xn = _rms(x, g_ref[...]).astype(BF16)
    g = _dot(xn, wg_ref[...])
    u = _dot(xn, wu_ref[...])
    h = (_silu(g) * u).astype(BF16)
    o_ref[...] = x + 0.5 * _dot(h, wd_ref[...])


def _ffn(x, gain, w_gu, w_down, mix=None):
    t, d = x.shape
    once = dict(pipeline_mode=pl.Buffered(1))
    row = pl.BlockSpec((ROW_TILE, d), lambda i: (i, 0))
    in_specs = [row]
    args = [x]
    prompt_steps = 0
    if mix is not None:
        prompt_steps = mix[0].shape[0] // ROW_TILE
        in_specs += [pl.BlockSpec((ROW_TILE, d), lambda i: (jnp.minimum(i, prompt_steps - 1), 0)),
                     pl.BlockSpec((ROW_TILE, d), lambda i: (jnp.maximum(i - prompt_steps, 0), 0)),
                     pl.BlockSpec((d, d), lambda i: (0, 0), **once)]
        args += list(mix)
    in_specs += [
        pl.BlockSpec((1, d), lambda i: (0, 0)),
        pl.BlockSpec((d, D_FF), lambda i: (0, 0), **once),
        pl.BlockSpec((d, D_FF), lambda i: (0, 1), **once),
        pl.BlockSpec((D_FF, d), lambda i: (0, 0), **once),
    ]
    return pl.pallas_call(
        functools.partial(_ffn_kernel, fused=mix is not None, prompt_steps=prompt_steps),
        out_shape=jax.ShapeDtypeStruct((t, d), F32),
        grid=(t // ROW_TILE,),
        in_specs=in_specs,
        out_specs=row,
        compiler_params=_params("parallel"),
        name="ffn",
    )(*args, gain.reshape(1, d), w_gu, w_gu, w_down)


def _pair_head_norm(x, gain2):
    low = _iota2(x.shape, 1) < BAND_DH
    sq = x * x
    s_lo = jnp.sum(jnp.where(low, sq, 0.0), axis=-1, keepdims=True)
    s_hi = jnp.sum(jnp.where(low, 0.0, sq), axis=-1, keepdims=True)
    ms = jnp.where(low, s_lo, s_hi) * (1.0 / BAND_DH)
    return x * lax.rsqrt(ms + EPS) * gain2


def _proj_kernel(*refs, segs, n_gain):
    x_ref, g_ref, w_ref = refs[:3]
    gains = refs[3:3 + n_gain]
    outs = refs[3 + n_gain:]
    xn = _rms(x_ref[...], g_ref[...]).astype(BF16)
    for c0, c1, gain_idx, dsts in segs:
        y = _dot(xn, w_ref[:, c0:c1])
        if gain_idx is not None:
            y = jnp.concatenate([_pair_head_norm(y[:, s:s + LANES], gains[gain_idx][...])
                                 for s in range(0, c1 - c0, LANES)], axis=1)
        for out_idx, off, scale in dsts:
            o_ref = outs[out_idx]
            o_ref[:, off:off + c1 - c0] = (y if scale == 1.0 else y * scale).astype(o_ref.dtype)


def _proj(x, gain, w, out_defs, segs, gains=(), rows=None):
    d = x.shape[1]
    r0, r1 = rows or (0, x.shape[0])
    first = r0 // ROW_TILE
    const = lambda shape: pl.BlockSpec(shape, lambda i: (0, 0))
    return pl.pallas_call(
        functools.partial(_proj_kernel, segs=tuple(segs), n_gain=len(gains)),
        out_shape=tuple(jax.ShapeDtypeStruct((r1 - r0, width), dt) for width, dt in out_defs),
        grid=((r1 - r0) // ROW_TILE,),
        in_specs=[pl.BlockSpec((ROW_TILE, d), lambda i: (first + i, 0)), const((1, d)),
                  pl.BlockSpec(w.shape, lambda i: (0, 0), pipeline_mode=pl.Buffered(1))]
        + [const((1, LANES))] * len(gains),
        out_specs=tuple(pl.BlockSpec((ROW_TILE, width), lambda i: (i, 0)) for width, _ in out_defs),
        compiler_params=_params("parallel"),
        name="proj",
    )(x, gain.reshape(1, d), w, *[jnp.tile(g, 2).reshape(1, LANES) for g in gains])


def _col_segs(n, dsts_of):
    return [(c, min(c + D_MODEL, n), None, dsts_of(c)) for c in range(0, n, D_MODEL)]


def _gla_kernel(*refs, has_init, group):
    if has_init:
        (q_ref, k_ref, v_ref, r_ref, gl_ref, wg2_ref, bg_ref, on_ref, s0_ref,
         o_ref, sfin_ref, st_ref) = refs
    else:
        (q_ref, k_ref, v_ref, r_ref, gl_ref, wg2_ref, bg_ref, on_ref,
         o_ref, sfin_ref, st_ref) = refs
    c = pl.program_id(1)
    units = [(g, h) for g in range(group) for h in range(GLA_HEADS)]

    @pl.when(c == 0)
    def _():
        for g, h in units:
            if has_init:
                st_ref[g * GLA_HEADS + h] = s0_ref[g, h].T
            else:
                st_ref[g * GLA_HEADS + h] = jnp.zeros((GLA_DV, GLA_DK), F32)

    incl = _iota2((CHUNK, CHUNK), 0) >= _iota2((CHUNK, CHUNK), 1)
    tri = jnp.where(incl, 1.0, 0.0).astype(BF16)
    q_e, k_e, k_l, d_last = [], [], [], []
    for g in range(group):
        x = _dot(gl_ref[g].astype(BF16), wg2_ref[...]) + bg_ref[...]
        log_a = _log_sigmoid(x) * (1.0 / GLA_TAU)
        b = _tri_dot(tri, log_a)
        b_last = b[CHUNK - 1:CHUNK, :]
        q = q_ref[g] * (GLA_DK ** -0.5)
        k = k_ref[g]
        q_e.append((q * jnp.exp(b)).astype(BF16))
        k_e.append((k * jnp.exp(-b)).astype(BF16))
        k_l.append((k * jnp.exp(b_last - b)).astype(BF16))
        d_last.append(jnp.exp(b_last))
    gain = on_ref[...]
    ks = [slice(h * GLA_DK, (h + 1) * GLA_DK) for h in range(GLA_HEADS)]
    vs = [slice(h * GLA_DV, (h + 1) * GLA_DV) for h in range(GLA_HEADS)]
    v = [v_ref[g, :, vs[h]].astype(BF16) for g, h in units]
    st = [st_ref[g * GLA_HEADS + h] for g, h in units]
    att = [jnp.where(incl, _dot_nt(q_e[g][:, ks[h]], k_e[g][:, ks[h]]), 0.0).astype(BF16) for g, h in units]
    inter = [_dot_nt(q_e[g][:, ks[h]], st[u].astype(BF16)) for u, (g, h) in enumerate(units)]
    o = [_dot(att[u], v[u]) + inter[u] for u in range(len(units))]
    upd = [_dot_tn(v[u], k_l[g][:, ks[h]]) for u, (g, h) in enumerate(units)]
    for u, (g, h) in enumerate(units):
        st_ref[g * GLA_HEADS + h] = st[u] * d_last[g][:, ks[h]] + upd[u]
        o_ref[g, :, vs[h]] = (_rms(o[u], gain) * _silu(r_ref[g, :, vs[h]])).astype(BF16)

    @pl.when(c == pl.num_programs(1) - 1)
    def _():
        for g, h in units:
            sfin_ref[g, h] = st_ref[g * GLA_HEADS + h].T


def _gla(proj, w_gate2, b_gate, onorm, s0, nb, nc):
    qk = GLA_HEADS * GLA_DK
    vv = GLA_HEADS * GLA_DV
    proj = proj.reshape(nb, nc * CHUNK, proj.shape[1])
    blk = lambda width, col: pl.BlockSpec((GROUP, CHUNK, width), lambda g, c: (g, c, col))
    in_specs = [
        blk(qk, 0), blk(qk, 1), blk(vv, 1), blk(vv, 2), blk(LANES, (2 * qk + 2 * vv) // LANES),
        pl.BlockSpec((LANES, qk), lambda g, c: (0, 0)),
        pl.BlockSpec((1, qk), lambda g, c: (0, 0)),
        pl.BlockSpec((1, GLA_DV), lambda g, c: (0, 0)),
    ]
    args = [proj, proj, proj, proj, proj, w_gate2, b_gate, onorm]
    state = pl.BlockSpec((GROUP, GLA_HEADS, GLA_DK, GLA_DV), lambda g, c: (g, 0, 0, 0))
    if s0 is not None:
        in_specs.append(state)
        args.append(s0)
    o, s_fin = pl.pallas_call(
        functools.partial(_gla_kernel, has_init=s0 is not None, group=GROUP),
        out_shape=(jax.ShapeDtypeStruct((nb, nc * CHUNK, vv), BF16),
                   jax.ShapeDtypeStruct((nb, GLA_HEADS, GLA_DK, GLA_DV), F32)),
        grid=(nb // GROUP, nc),
        in_specs=in_specs,
        out_specs=(blk(vv, 0), state),
        scratch_shapes=[pltpu.VMEM((GROUP * GLA_HEADS, GLA_DV, GLA_DK), F32)],
        compiler_params=_params("parallel", "arbitrary"),
        name="gla",
    )(*args)
    return o.reshape(nb * nc * CHUNK, vv), s_fin


BIAS_WIDTH = 640


def _band_bias_kernel(w_ref, o_ref):
    for h in range(BAND_HEADS):
        x = jnp.broadcast_to(w_ref[h:h + 1, :], (CHUNK, BIAS_WIDTH))
        rolled = pltpu.roll(x, 0, 1, stride=1, stride_axis=0)
        o_ref[h // 2, (h % 2) * CHUNK:(h % 2 + 1) * CHUNK, :] = rolled[:, 0:BAND_SPAN]


def _band_bias(rel_bias):
    c = jnp.arange(BIAS_WIDTH)
    n = jnp.where(c <= BAND_SPAN, c, c - BIAS_WIDTH)
    w = rel_bias[:, jnp.clip(BAND_WINDOW - n, -REL_CLIP, REL_CLIP) + REL_CLIP]
    return pl.pallas_call(
        _band_bias_kernel,
        out_shape=jax.ShapeDtypeStruct((BAND_HEADS // 2, 2 * CHUNK, BAND_SPAN), F32),
        name="band_bias",
    )(w)


def _band_attn_kernel(q_ref, kp_ref, kc_ref, vp_ref, vc_ref, bias_ref, o_ref, kall_ref, vall_ref,
                      *, nq, prompt):
    if prompt:
        kall_ref[0:BAND_WINDOW] = kp_ref[...]
        vall_ref[0:BAND_WINDOW] = vp_ref[...]
    else:
        kall_ref[0:BAND_WINDOW] = kp_ref[0].astype(BF16)
        vall_ref[0:BAND_WINDOW] = vp_ref[0].astype(BF16)
    kall_ref[BAND_WINDOW:] = kc_ref[...]
    vall_ref[BAND_WINDOW:] = vc_ref[...]
    first_pos = (pl.program_id(1) - 1) * BAND_WINDOW if prompt else 0
    key_idx = _iota2((2 * CHUNK, BAND_SPAN), 1)
    pairs = range(D_MODEL // LANES)
    lanes = [slice(p * LANES, (p + 1) * LANES) for p in pairs]

    def chunk(t, carry):
        r0 = pl.multiple_of(t * CHUNK, CHUNK)
        qs = [_stack_pair(q_ref[pl.ds(r0, CHUNK), ls]) for ls in lanes]
        s = [_dot_nt(qs[p], kall_ref[pl.ds(r0, BAND_SPAN), lanes[p]]) * (BAND_DH ** -0.5) + bias_ref[p]
             for p in pairs]
        if prompt:
            visible = first_pos + r0 + key_idx >= 0
            s = [jnp.where(visible, x, -jnp.inf) for x in s]
        e = [jnp.exp(x - jnp.max(x, axis=-1, keepdims=True)) for x in s]
        prob = [(x / jnp.sum(x, axis=-1, keepdims=True)).astype(BF16) for x in e]
        pv = [_dot(prob[p], vall_ref[pl.ds(r0, BAND_SPAN), lanes[p]]) for p in pairs]
        for p in pairs:
            o_ref[pl.ds(r0, CHUNK), lanes[p]] = _unstack_pair(pv[p]).astype(BF16)
        return carry

    lax.fori_loop(0, nq, chunk, 0)


def _band_attn_prompt(qn, knb, vb, bias, nb, seq):
    d = D_MODEL
    blk = BAND_WINDOW
    nblk = seq // blk
    cur = pl.BlockSpec((blk, d), lambda b, i: (b * nblk + i, 0))
    prev = pl.BlockSpec((blk, d), lambda b, i: (b * nblk + jnp.maximum(i - 1, 0), 0))
    return pl.pallas_call(
        functools.partial(_band_attn_kernel, nq=blk // CHUNK, prompt=True),
        out_shape=jax.ShapeDtypeStruct((nb * seq, d), BF16),
        grid=(nb, nblk),
        in_specs=[cur, prev, cur, prev, cur,
                  pl.BlockSpec(bias.shape, lambda b, i: (0, 0, 0))],
        out_specs=cur,
        scratch_shapes=[pltpu.VMEM((2 * blk, d), BF16), pltpu.VMEM((2 * blk, d), BF16)],
        compiler_params=_params("parallel", "arbitrary"),
        name="band_attn_prompt",
    )(qn, knb, knb, vb, vb, bias)


def _band_attn_sample(qn, knb, vb, cache_k, cache_v, bias, nb, row_off):
    d = D_MODEL
    new = pl.BlockSpec((CHUNK, d), lambda b: (row_off + b, 0))
    old = pl.BlockSpec((1, BAND_WINDOW, d), lambda b: (b, 0, 0))
    return pl.pallas_call(
        functools.partial(_band_attn_kernel, nq=1, prompt=False),
        out_shape=jax.ShapeDtypeStruct((nb * CHUNK, d), BF16),
        grid=(nb,),
        in_specs=[new, old, new, old, new, pl.BlockSpec(bias.shape, lambda b: (0, 0, 0))],
        out_specs=pl.BlockSpec((CHUNK, d), lambda b: (b, 0)),
        scratch_shapes=[pltpu.VMEM((BAND_SPAN, d), BF16), pltpu.VMEM((BAND_SPAN, d), BF16)],
        compiler_params=_params("parallel"),
        name="band_attn_sample",
    )(qn, cache_k, knb, cache_v, vb, bias)


def _gdn_prep_kernel(*refs, has_init):
    if has_init:
        x_ref, raw_ref, cw_ref, par_ref, c0_ref, q_ref, k_ref, v_ref, bg_ref, carry_ref = refs
    else:
        x_ref, raw_ref, cw_ref, par_ref, q_ref, k_ref, v_ref, bg_ref, carry_ref = refs
    tb = x_ref.shape[0]

    @pl.when(pl.program_id(1) == 0)
    def _():
        if has_init:
            carry_ref[...] = c0_ref[0]
        else:
            carry_ref[...] = jnp.zeros_like(carry_ref)

    x = x_ref[...]
    w = cw_ref[...]
    x8 = x[0:SUBLANES]
    c8 = carry_ref[...]
    row8 = _iota2(x8.shape, 0)
    y = x * w[CONV_W - 1:CONV_W]
    y8 = x8 * w[CONV_W - 1:CONV_W]
    for s in range(1, CONV_W):
        ws = w[CONV_W - 1 - s:CONV_W - s]
        y = y + pltpu.roll(x, s, 0) * ws
        y8 = y8 + jnp.where(row8 < s, pltpu.roll(c8, s, 0), pltpu.roll(x8, s, 0)) * ws
    carry_ref[...] = x[tb - SUBLANES:tb]
    if tb > SUBLANES:
        y = jnp.concatenate([y8, y[SUBLANES:]], axis=0)
    else:
        y = y8
    y = _silu(y)
    qk = GDN_HEADS * GDN_DK
    for h in range(GDN_HEADS):
        qh = y[:, h * GDN_DK:(h + 1) * GDN_DK]
        kh = y[:, qk + h * GDN_DK:qk + (h + 1) * GDN_DK]
        q_ref[h] = qh * lax.rsqrt(jnp.sum(qh * qh, axis=-1, keepdims=True) + EPS)
        k_ref[h] = kh * lax.rsqrt(jnp.sum(kh * kh, axis=-1, keepdims=True) + EPS)
        v_ref[h] = y[:, 2 * qk + h * GDN_DV:2 * qk + (h + 1) * GDN_DV]
    raw = raw_ref[...]
    beta = _sigmoid(raw)
    g = par_ref[0:1, :] * _softplus(raw + par_ref[1:2, :])
    bg_ref[...] = jnp.where(_iota2(raw.shape, 1) < GDN_HEADS, beta, g)


def _gdn_prep(proj, conv_w, par, c0, nb, seq, row_off, tb):
    nblk = seq // tb
    base = row_off // tb
    row = lambda b, i: base + b * nblk + i
    in_specs = [
        pl.BlockSpec((tb, GDN_CONV_CH), lambda b, i: (row(b, i), 0)),
        pl.BlockSpec((tb, LANES), lambda b, i: (row(b, i), (GDN_CONV_CH + GDN_HEADS * GDN_DV) // LANES)),
        pl.BlockSpec((CONV_W, GDN_CONV_CH), lambda b, i: (0, 0)),
        pl.BlockSpec((2, LANES), lambda b, i: (0, 0)),
    ]
    args = [proj, proj, conv_w, par]
    if c0 is not None:
        in_specs.append(pl.BlockSpec((1, SUBLANES, GDN_CONV_CH), lambda b, i: (b, 0, 0)))
        args.append(c0)
    n = nb * seq
    head = pl.BlockSpec((GDN_HEADS, tb, GDN_DK), lambda b, i: (0, b * nblk + i, 0))
    return pl.pallas_call(
        functools.partial(_gdn_prep_kernel, has_init=c0 is not None),
        out_shape=(jax.ShapeDtypeStruct((GDN_HEADS, n, GDN_DK), F32),
                   jax.ShapeDtypeStruct((GDN_HEADS, n, GDN_DK), F32),
                   jax.ShapeDtypeStruct((GDN_HEADS, n, GDN_DV), F32),
                   jax.ShapeDtypeStruct((n, LANES), F32)),
        grid=(nb, nblk),
        in_specs=in_specs,
        out_specs=(head, head, head, pl.BlockSpec((tb, LANES), lambda b, i: (b * nblk + i, 0))),
        scratch_shapes=[pltpu.VMEM((SUBLANES, GDN_CONV_CH), F32)],
        compiler_params=_params("parallel", "arbitrary"),
        name="gdn_prep",
    )(*args)


def _gdn_kernel(*refs, has_init, group):
    if has_init:
        q_ref, k_ref, v_ref, bg_ref, gate_ref, on_ref, s0_ref, o_ref, sfin_ref, s_ref = refs
    else:
        q_ref, k_ref, v_ref, bg_ref, gate_ref, on_ref, o_ref, sfin_ref, s_ref = refs
    c = pl.program_id(1)

    @pl.when(c == 0)
    def _():
        if has_init:
            s_ref[...] = s0_ref[...].reshape(s_ref.shape)
        else:
            s_ref[...] = jnp.zeros_like(s_ref)

    ri = _iota2((CHUNK, CHUNK), 0)
    ci = _iota2((CHUNK, CHUNK), 1)
    incl = ri >= ci
    strict = ri > ci
    tri = jnp.where(incl, 1.0, 0.0).astype(BF16)
    eye = jnp.where(ri == ci, 1.0, 0.0)
    gain = on_ref[...]
    units = [(g, h) for g in range(group) for h in range(GDN_HEADS)]
    heads = range(len(units))
    bg = [bg_ref[g] for g in range(group)]
    cs = [_tri_dot(tri, x) for x in bg]
    cs_t = [x.T for x in cs]
    beta = [bg[g][:, h:h + 1] for g, h in units]
    gcol = [cs[g][:, GDN_HEADS + h:GDN_HEADS + h + 1] for g, h in units]
    glast = [x[CHUNK - 1:CHUNK, :] for x in gcol]
    gam = [jnp.exp(jnp.where(incl, gcol[u] - cs_t[g][GDN_HEADS + h:GDN_HEADS + h + 1, :], -jnp.inf))
           for u, (g, h) in enumerate(units)]
    egc = [jnp.exp(x) for x in gcol]
    k = [k_ref[h, g] for g, h in units]
    k16 = [x.astype(BF16) for x in k]
    kb = [k[h] * beta[h] for h in heads]
    m = [jnp.where(strict, _dot_nt(kb[h].astype(BF16), k16[h]) * gam[h], 0.0) for h in heads]
    d = [eye - jnp.where((ri == ci + 1) & ((ci & 1) == 0), m[h], 0.0) for h in heads]
    for lvl in range(1, 6):
        sub = (((ri >> lvl) & 1) == 1) & (((ci >> lvl) & 1) == 0) & ((ri >> (lvl + 1)) == (ci >> (lvl + 1)))
        y = [_dot_f32(d[h], jnp.where(sub, m[h], 0.0)) for h in heads]
        x = [_dot_f32(y[h], d[h]) for h in heads]
        d = [d[h] - x[h] for h in heads]
    t16 = [t.astype(BF16) for t in d]
    u = [_dot(t16[i], (v_ref[h, g] * beta[i]).astype(BF16)) for i, (g, h) in enumerate(units)]
    w = [_dot(t16[h], (kb[h] * egc[h]).astype(BF16)) for h in heads]
    q = [q_ref[h, g] * (GDN_DK ** -0.5) for g, h in units]
    att = [_dot_nt(q[h].astype(BF16), k16[h]) * gam[h] for h in heads]
    q_e = [(q[h] * egc[h]).astype(BF16) for h in heads]
    k_l = [(k[h] * jnp.exp(glast[h] - gcol[h])).astype(BF16) for h in heads]
    s = [s_ref[h] for h in heads]
    s16 = [x.astype(BF16) for x in s]
    v_new = [u[h] - _dot(w[h].astype(BF16), s16[h]) for h in heads]
    v16 = [x.astype(BF16) for x in v_new]
    o = [_dot(q_e[h], s16[h]) + _dot(att[h].astype(BF16), v16[h]) for h in heads]
    for i, (g, h) in enumerate(units):
        s_ref[i] = s[i] * jnp.exp(glast[i]) + _dot_tn(k_l[i], v16[i])
        vs = slice(h * GDN_DV, (h + 1) * GDN_DV)
        o_ref[g, :, vs] = (_rms(o[i], gain) * _silu(gate_ref[g, :, vs])).astype(BF16)

    @pl.when(c == pl.num_programs(1) - 1)
    def _():
        sfin_ref[...] = s_ref[...].reshape(sfin_ref.shape)


def _gdn(qh, kh, vh, bg, proj, onorm, s0, nb, nc):
    vv = GDN_HEADS * GDN_DV
    seq = nc * CHUNK
    qh, kh, vh = (t.reshape(GDN_HEADS, nb, seq, t.shape[-1]) for t in (qh, kh, vh))
    bg = bg.reshape(nb, seq, LANES)
    proj = proj.reshape(nb, seq, proj.shape[1])
    head = pl.BlockSpec((GDN_HEADS, GROUP, CHUNK, GDN_DK), lambda g, c: (0, g, c, 0))
    in_specs = [
        head, head, head,
        pl.BlockSpec((GROUP, CHUNK, LANES), lambda g, c: (g, c, 0)),
        pl.BlockSpec((GROUP, CHUNK, vv), lambda g, c: (g, c, GDN_CONV_CH // vv)),
        pl.BlockSpec((1, GDN_DV), lambda g, c: (0, 0)),
    ]
    args = [qh, kh, vh, bg, proj, onorm]
    state = pl.BlockSpec((GROUP, GDN_HEADS, GDN_DK, GDN_DV), lambda g, c: (g, 0, 0, 0))
    if s0 is not None:
        in_specs.append(state)
        args.append(s0)
    o, s_fin = pl.pallas_call(
        functools.partial(_gdn_kernel, has_init=s0 is not None, group=GROUP),
        out_shape=(jax.ShapeDtypeStruct((nb, seq, vv), BF16),
                   jax.ShapeDtypeStruct((nb, GDN_HEADS, GDN_DK, GDN_DV), F32)),
        grid=(nb // GROUP, nc),
        in_specs=in_specs,
        out_specs=(pl.BlockSpec((GROUP, CHUNK, vv), lambda g, c: (g, c, 0)), state),
        scratch_shapes=[pltpu.VMEM((GROUP * GDN_HEADS, GDN_DK, GDN_DV), F32)],
        compiler_params=_params("parallel", "arbitrary"),
        name="gdn",
    )(*args)
    return o.reshape(nb * seq, vv), s_fin


def _neg_abs(x):
    bits = lax.bitcast_convert_type(x, jnp.uint32) | jnp.uint32(0x80000000)
    return lax.bitcast_convert_type(bits, F32)


def _stack_pair(q2):
    low = _iota2(q2.shape, 1) < SB_DH
    return jnp.concatenate([jnp.where(low, q2, 0), jnp.where(low, 0, q2)], axis=0)


def _unstack_pair(acc):
    tq = acc.shape[0] // 2
    low = _iota2((tq, LANES), 1) < SB_DH
    return jnp.where(low, acc[:tq], acc[tq:])


def _upper(tk):
    return jnp.where(_iota2((tk, tk), 0) > _iota2((tk, tk), 1), 1.0, 0.0).astype(BF16)


def _diag_mask(tq, tk):
    r = _iota2((2 * tq, tk), 0)
    r = jnp.where(r >= tq, r - tq, r)
    return _iota2((2 * tq, tk), 1) < r


SB_NEUTRAL = 1e30
SB_LAG = 4
SB_CHAINS = 1


def _sb_log_terms(zn, mask):
    l1m = jnp.minimum(zn, 0.0) - jnp.log2(1.0 + jnp.exp2(_neg_abs(zn)))
    log_b = l1m - zn
    if mask is not None:
        l1m = jnp.where(mask, l1m, 0.0)
        log_b = jnp.where(mask, log_b, -SB_NEUTRAL)
    return l1m, log_b


def _sb_direct(qs, k2, v2, mask):
    l1m, log_b = _sb_log_terms(_dot_nt(qs, k2), mask)
    a = jnp.exp2(log_b + _dot(l1m.astype(BF16), _upper(k2.shape[0])))
    return jnp.sum(l1m, axis=-1, keepdims=True), _dot(a.astype(BF16), v2)


def _sb_pipeline(nhalf, nchain, q_of, k_of, v_of, st_of, upper, scratch, mask):
    z_ref, hi_ref, logb_ref, tot_ref, cs_ref, a_ref, run_ref, acc_ref = scratch
    logb_ref[...] = jnp.full(logb_ref.shape, -SB_NEUTRAL, F32)
    tot_ref[...] = jnp.zeros(tot_ref.shape, F32)
    for c in range(nchain):
        z_ref[2 * c + 1] = jnp.full(z_ref.shape[1:], SB_NEUTRAL, F32)
        hi_ref[2 * c + 1] = jnp.zeros(hi_ref.shape[1:], BF16)
        cs_ref[2 * c + 1] = jnp.zeros(cs_ref.shape[1:], F32)
        a_ref[2 * c + 1] = jnp.zeros(a_ref.shape[1:], BF16)

    def half(h, p):
        for c in range(nchain):
            rows = st_of(h - 4, c)
            acc_ref[rows, :] += _dot(a_ref[2 * c + 1 - p], v_of(h - 4, c))
        for c in range(nchain):
            rows = st_of(h - 3, c)
            run = run_ref[rows, :]
            a_ref[2 * c + p] = jnp.exp2(cs_ref[2 * c + 1 - p] + (logb_ref[2 * c + p] + run)).astype(BF16)
            run_ref[rows, :] = run + tot_ref[2 * c + p]
        for c in range(nchain):
            cs_ref[2 * c + p] = _dot(hi_ref[2 * c + 1 - p], upper)
        for c in range(nchain):
            l1m, logb_ref[2 * c + p] = _sb_log_terms(z_ref[2 * c + 1 - p], mask)
            hi_ref[2 * c + p] = l1m.astype(BF16)
            tot_ref[2 * c + p] = jnp.sum(l1m, axis=-1, keepdims=True)
        for c in range(nchain):
            z_ref[2 * c + p] = _dot_nt(q_of(h, c), k_of(h, c))

    def body(i, _):
        half(2 * i, 0)
        half(2 * i + 1, 1)
        return 0

    assert nhalf % 2 == 0
    lax.fori_loop(0, nhalf // 2, body, 0)


def _sb_rings(nchain, rows, tk):
    slots = 2 * nchain
    return [pltpu.VMEM((slots, rows, tk), F32), pltpu.VMEM((slots, rows, tk), BF16),
            pltpu.VMEM((slots, rows, tk), F32), pltpu.VMEM((slots, rows, 1), F32),
            pltpu.VMEM((slots, rows, tk), F32), pltpu.VMEM((slots, rows, tk), BF16)]


def _sb_prompt_items(nq):
    diag = [(i, i) for i in range(nq)]
    off = [(j, i) for j in range(nq - 2, -1, -1) for i in range(j + 1, nq)]
    pad = [(0, nq)] * (SB_CHAINS * SB_LAG)
    assert len(diag) % (2 * SB_CHAINS) == 0 and len(off) % (2 * SB_CHAINS) == 0
    rows = []
    for items in (diag, off):
        padded = pad + items + pad
        rows += [[j for j, _ in padded], [i for _, i in padded]]
    width = max(len(r) for r in rows)
    tab = np.zeros((4, width), np.int32)
    for r, row in enumerate(rows):
        tab[r, :len(row)] = row
    return jnp.asarray(tab), len(diag), len(off)


def _sb_prompt_kernel(tab_ref, q_ref, k_ref, v_ref, o_ref, qs_ref, run_ref, acc_ref, *rings, nq, ndiag, noff):
    tk = SB_TK
    rows = 2 * tk

    def stack(i, _):
        qs_ref[pl.ds(pl.multiple_of(i * rows, rows), rows), :] = _stack_pair(
            q_ref[pl.ds(pl.multiple_of(i * tk, tk), tk), :])
        return 0

    lax.fori_loop(0, nq, stack, 0)
    qs_ref[nq * rows:(nq + 1) * rows, :] = jnp.zeros((rows, LANES), BF16)
    run_ref[...] = jnp.zeros(run_ref.shape, F32)
    acc_ref[...] = jnp.zeros(acc_ref.shape, F32)
    upper = _upper(tk)
    scratch = list(rings) + [run_ref, acc_ref]
    for row, nitems, mask in ((0, ndiag, _diag_mask(tk, tk)), (2, noff, None)):
        col = lambda hh, c: SB_CHAINS * (hh + SB_LAG) + c
        q_rows = lambda hh, c: pl.ds(pl.multiple_of(tab_ref[row + 1, col(hh, c)] * rows, rows), rows)
        k_rows = lambda hh, c: pl.ds(pl.multiple_of(tab_ref[row, col(hh, c)] * tk, tk), tk)
        _sb_pipeline(nitems // SB_CHAINS + SB_LAG, SB_CHAINS,
                     lambda hh, c: qs_ref[q_rows(hh, c), :],
                     lambda hh, c: k_ref[k_rows(hh, c), :],
                     lambda hh, c: v_ref[k_rows(hh, c), :],
                     q_rows, upper, scratch, mask)

    def emit(i, _):
        o_ref[pl.ds(pl.multiple_of(i * tk, tk), tk), :] = _unstack_pair(
            acc_ref[pl.ds(pl.multiple_of(i * rows, rows), rows), :]).astype(BF16)
        return 0

    lax.fori_loop(0, nq, emit, 0)


def _sb_prompt(qb, kb, vb, nb, seq):
    d = D_MODEL
    npair = d // LANES
    nq = seq // SB_TK
    rows = 2 * SB_TK
    tab, ndiag, noff = _sb_prompt_items(nq)
    col = pl.BlockSpec((seq, LANES), lambda b, p, tab: (b, p))
    return pl.pallas_call(
        functools.partial(_sb_prompt_kernel, nq=nq, ndiag=ndiag, noff=noff),
        out_shape=jax.ShapeDtypeStruct((nb * seq, d), BF16),
        grid_spec=pltpu.PrefetchScalarGridSpec(
            num_scalar_prefetch=1,
            grid=(nb, npair),
            in_specs=[col, col, col],
            out_specs=col,
            scratch_shapes=[pltpu.VMEM(((nq + 1) * rows, LANES), BF16), pltpu.VMEM(((nq + 1) * rows, 1), F32),
                            pltpu.VMEM(((nq + 1) * rows, LANES), F32)] + _sb_rings(SB_CHAINS, rows, SB_TK)),
        compiler_params=_params("parallel", "arbitrary"),
        name="sb_prompt",
    )(tab, qb, kb, vb)


def _sb_sample_kernel(q_ref, kn_ref, vn_ref, kc_ref, vc_ref, o_ref, run_ref, acc_ref, *rings):
    tq, tk = CHUNK, SB_TK
    rows = 2 * tq
    nblk = kc_ref.shape[1] // tk
    lanes = lambda c: slice(c * LANES, (c + 1) * LANES)
    qs = [_stack_pair(q_ref[:, lanes(c)]) for c in range(2)]
    for c in range(2):
        run, acc = _sb_direct(qs[c], kn_ref[:, lanes(c)], vn_ref[:, lanes(c)], _diag_mask(tq, tq))
        run_ref[c * rows:(c + 1) * rows, :] = run
        acc_ref[c * rows:(c + 1) * rows, :] = acc

    def k_rows(hh):
        return pl.ds(pl.multiple_of(jnp.clip(nblk - 1 - hh, 0, nblk - 1) * tk, tk), tk)

    def st_of(hh, c):
        real = (hh >= 0) & (hh < nblk)
        return pl.ds(pl.multiple_of(jnp.where(real, c, 2) * rows, rows), rows)

    _sb_pipeline(nblk + SB_LAG, 2,
                 lambda hh, c: qs[c],
                 lambda hh, c: kc_ref[0, k_rows(hh), lanes(c)].astype(BF16),
                 lambda hh, c: vc_ref[0, k_rows(hh), lanes(c)].astype(BF16),
                 st_of, _upper(tk), list(rings) + [run_ref, acc_ref], None)
    for c in range(2):
        o_ref[:, lanes(c)] = _unstack_pair(acc_ref[c * rows:(c + 1) * rows, :]).astype(BF16)


def _sb_sample(qb, kb, vb, cache_k, cache_v, nb, row_off):
    d = D_MODEL
    wide = 2 * LANES
    ngrp = d // wide
    past = cache_k.shape[1]
    rows = 2 * CHUNK
    new = pl.BlockSpec((CHUNK, wide), lambda b, g: (row_off + b, g))
    old = pl.BlockSpec((1, past, wide), lambda b, g: (b, 0, g))
    return pl.pallas_call(
        _sb_sample_kernel,
        out_shape=jax.ShapeDtypeStruct((nb * CHUNK, d), BF16),
        grid=(nb, ngrp),
        in_specs=[new, new, new, old, old],
        out_specs=pl.BlockSpec((CHUNK, wide), lambda b, g: (b, g)),
        scratch_shapes=[pltpu.VMEM((3 * rows, 1), F32), pltpu.VMEM((3 * rows, LANES), F32)]
        + _sb_rings(2, rows, SB_TK),
        compiler_params=_params("parallel", "parallel"),
        name="sb_sample",
    )(qb, kb, vb, cache_k, cache_v)


def _pad_cols(w, n):
    return jnp.pad(w, ((0, 0), (0, n - w.shape[1])))


def kernel(x_prompt, x_sample, state_gla, cache_band_k, cache_band_v, state_gdn, state_gdn_conv,
           cache_sb_k, cache_sb_v, ffn1_norm, ffn1_w_gu, ffn1_w_down, mix_norm, ffn2_norm,
           ffn2_w_gu, ffn2_w_down, gla_w_in, gla_w_gate2, gla_b_gate, gla_onorm, gla_w_out,
           band_w_in, band_q_norm, band_k_norm, band_rel_bias, band_w_out, gdn_w_in, gdn_conv_w,
           gdn_a_log, gdn_dt_bias, gdn_onorm, gdn_w_out, sb_w_in, sb_w_out):
    bp, seq, d = x_prompt.shape
    bs, dec = x_sample.shape[:2]
    assert dec == CHUNK and seq % BAND_WINDOW == 0 and d == D_MODEL
    n_p = bp * seq
    n_s = bs * dec
    nc_p = seq // CHUNK
    x = jnp.concatenate([x_prompt.reshape(n_p, d), x_sample.reshape(n_s, d)], axis=0)
    depth = ffn1_norm.shape[0]
    outs = {}
    for i in range(depth):
        m, j = i % 4, i // 4
        x = _ffn(x, ffn1_norm[i], ffn1_w_gu[i].astype(BF16), ffn1_w_down[i].astype(BF16))
        if m == 0:
            n_in = 2 * GLA_HEADS * GLA_DK + 2 * GLA_HEADS * GLA_DV + LANES
            gla_proj = functools.partial(_proj, x, mix_norm[i], _pad_cols(gla_w_in[j], n_in).astype(BF16),
                                         [(n_in, F32)], _col_segs(n_in, lambda c: [(0, c, 1.0)]))
            wg2 = jnp.pad(gla_w_gate2[j], ((0, LANES - GLA_RANK), (0, 0))).astype(BF16)
            common = (wg2, gla_b_gate[j].reshape(1, -1), gla_onorm[j].reshape(1, -1))
            o_p, s_p = _gla(*gla_proj(rows=(0, n_p)), *common, None, bp, nc_p)
            o_s, s_s = _gla(*gla_proj(rows=(n_p, n_p + n_s)), *common, state_gla[j], bs, 1)
            outs.setdefault("gla", []).append((s_p, s_s))
            w_out = gla_w_out[j]
        elif m == 1:
            band_proj = functools.partial(
                _proj, x, mix_norm[i], band_w_in[j].astype(BF16),
                [(d, BF16), (d, F32), (d, BF16), (d, F32), (d, BF16)],
                [(0, d, 0, [(0, 0, 1.0)]), (d, 2 * d, 1, [(1, 0, 1.0), (2, 0, 1.0)]),
                 (2 * d, 3 * d, None, [(3, 0, 1.0), (4, 0, 1.0)])],
                gains=(band_q_norm[j], band_k_norm[j]))
            qn_p, kn_p, knb_p, v_p, vb_p = band_proj(rows=(0, n_p))
            qn_s, kn_s, knb_s, v_s, vb_s = band_proj(rows=(n_p, n_p + n_s))
            bias = _band_bias(band_rel_bias[j])
            o_p = _band_attn_prompt(qn_p, knb_p, vb_p, bias, bp, seq)
            o_s = _band_attn_sample(qn_s, knb_s, vb_s, cache_band_k[j].reshape(bs, BAND_WINDOW, d),
                                    cache_band_v[j].reshape(bs, BAND_WINDOW, d), bias, bs, 0)
            keep = min(BAND_WINDOW, seq)
            shp = (BAND_HEADS, BAND_DH)
            tail = lambda a: jnp.stack([a[(b + 1) * seq - keep:(b + 1) * seq] for b in range(bp)]).reshape(bp, keep, *shp)
            outs.setdefault("band", []).append((
                tail(kn_p), tail(v_p), kn_s.reshape(bs, dec, *shp), v_s.reshape(bs, dec, *shp)))
            w_out = band_w_out[j]
        elif m == 2:
            n_in = GDN_CONV_CH + GDN_HEADS * GDN_DV + LANES
            gdn_proj = functools.partial(_proj, x, mix_norm[i], _pad_cols(gdn_w_in[j], n_in).astype(BF16),
                                         [(n_in, F32)], _col_segs(n_in, lambda c: [(0, c, 1.0)]))
            (proj_p,), (proj_s,) = gdn_proj(rows=(0, n_p)), gdn_proj(rows=(n_p, n_p + n_s))
            par = jnp.zeros((2, LANES), F32)
            par = par.at[0, GDN_HEADS:2 * GDN_HEADS].set(-jnp.exp(gdn_a_log[j]))
            par = par.at[1, GDN_HEADS:2 * GDN_HEADS].set(gdn_dt_bias[j])
            c0 = jnp.pad(state_gdn_conv[j], ((0, 0), (SUBLANES - (CONV_W - 1), 0), (0, 0)))
            pre_p = _gdn_prep(proj_p, gdn_conv_w[j], par, None, bp, seq, 0, ROW_TILE)
            pre_s = _gdn_prep(proj_s, gdn_conv_w[j], par, c0, bs, dec, 0, dec)
            onorm = gdn_onorm[j].reshape(1, -1)
            o_p, s_p = _gdn(*pre_p, proj_p, onorm, None, bp, nc_p)
            o_s, s_s = _gdn(*pre_s, proj_s, onorm, state_gdn[j], bs, 1)
            last = CONV_W - 1
            conv_p = jnp.stack([proj_p[(b + 1) * seq - last:(b + 1) * seq, :GDN_CONV_CH] for b in range(bp)])
            conv_s = proj_s[:, :GDN_CONV_CH].reshape(bs, dec, -1)[:, dec - last:]
            outs.setdefault("gdn", []).append((s_p, conv_p, s_s, conv_s))
            w_out = gdn_w_out[j]
        else:
            sb_proj = functools.partial(
                _proj, x, mix_norm[i], sb_w_in[j].astype(BF16),
                [(d, BF16), (d, F32), (d, BF16), (d, F32), (d, BF16)],
                [(0, d, None, [(0, 0, -LOG2E * SB_DH ** -0.5)]), (d, 2 * d, None, [(1, 0, 1.0), (2, 0, 1.0)]),
                 (2 * d, 3 * d, None, [(3, 0, 1.0), (4, 0, 1.0)])])
            qb_p, k_p, kb_p, v_p, vb_p = sb_proj(rows=(0, n_p))
            qb_s, k_s, kb_s, v_s, vb_s = sb_proj(rows=(n_p, n_p + n_s))
            o_p = _sb_prompt(qb_p, kb_p, vb_p, bp, seq)
            o_s = _sb_sample(qb_s, kb_s, vb_s, cache_sb_k[j].reshape(bs, -1, d),
                             cache_sb_v[j].reshape(bs, -1, d), bs, 0)
            shp = (SB_HEADS, SB_DH)
            outs.setdefault("sb", []).append((
                k_p.reshape(bp, seq, *shp), v_p.reshape(bp, seq, *shp),
                k_s.reshape(bs, dec, *shp), v_s.reshape(bs, dec, *shp)))
            w_out = sb_w_out[j]
        x = _ffn(x, ffn2_norm[i], ffn2_w_gu[i].astype(BF16), ffn2_w_down[i].astype(BF16),
                 mix=(o_p, o_s, w_out.astype(BF16)))
    stack = lambda key, idx: jnp.stack([t[idx] for t in outs[key]])
    return (x[:n_p].reshape(bp, seq, d), x[n_p:].reshape(bs, dec, d),
            stack("gla", 0), stack("gla", 1),
            stack("band", 0), stack("band", 1), stack("band", 2), stack("band", 3),
            stack("gdn", 0), stack("gdn", 1), stack("gdn", 2), stack("gdn", 3),
            stack("sb", 0), stack("sb", 1), stack("sb", 2), stack("sb", 3))
```

```python
import functools

import jax
import jax.numpy as jnp
import numpy as np
from jax import lax
from jax.experimental import pallas as pl
from jax.experimental.pallas import tpu as pltpu

F32 = jnp.float32
BF16 = jnp.bfloat16

D_MODEL = 1024
CHUNK = 64
EPS = 1e-6
D_FF = 2816
GLA_HEADS, GLA_DK, GLA_DV, GLA_RANK, GLA_TAU = 4, 128, 256, 16, 16.0
BAND_HEADS, BAND_DH, LEFT_CHUNKS, REL_CLIP = 16, 64, 8, 256
BAND_WINDOW = LEFT_CHUNKS * CHUNK
BAND_SPAN = BAND_WINDOW + CHUNK
GDN_HEADS, GDN_DK, GDN_DV, CONV_W = 8, 128, 128, 4
GDN_CONV_CH = 2 * GDN_HEADS * GDN_DK + GDN_HEADS * GDN_DV
SB_HEADS, SB_DH = 16, 64
LOG2E = 1.4426950408889634

LANES = 128
SUBLANES = 8
VMEM_LIMIT = 56 * 1024 * 1024
ROW_TILE = 512
SB_TK = 256
GROUP = 2


def _params(*sem):
    return pltpu.CompilerParams(dimension_semantics=sem, vmem_limit_bytes=VMEM_LIMIT)


def _dot(a, b):
    return jnp.dot(a, b, preferred_element_type=F32)


def _dot_nt(a, b):
    return lax.dot_general(a, b, (((1,), (1,)), ((), ())), preferred_element_type=F32)


def _dot_tn(a, b):
    return lax.dot_general(a, b, (((0,), (0,)), ((), ())), preferred_element_type=F32)


def _split3(x):
    h1 = x.astype(BF16)
    r1 = x - h1.astype(F32)
    h2 = r1.astype(BF16)
    h3 = (r1 - h2.astype(F32)).astype(BF16)
    return h1, h2, h3


def _tri_dot(tri, x):
    h1, h2, h3 = _split3(x)
    return _dot(tri, h1) + _dot(tri, h2) + _dot(tri, h3)


def _dot_f32(a, b):
    a1 = a.astype(BF16)
    a2 = (a - a1.astype(F32)).astype(BF16)
    b1 = b.astype(BF16)
    b2 = (b - b1.astype(F32)).astype(BF16)
    return _dot(a1, b1) + (_dot(a1, b2) + _dot(a2, b1))


def _sigmoid(x):
    return 1.0 / (1.0 + jnp.exp(-x))


def _silu(x):
    return x * _sigmoid(x)


def _softplus(x):
    return jnp.maximum(x, 0.0) + jnp.log1p(jnp.exp(-jnp.abs(x)))


def _log_sigmoid(x):
    return -_softplus(-x)


def _rms(x, g):
    return x * lax.rsqrt(jnp.mean(x * x, axis=-1, keepdims=True) + EPS) * g


def _iota2(shape, dim):
    return lax.broadcasted_iota(jnp.int32, shape, dim)


def _ffn_kernel(*refs, mode, prompt_steps):
    in_prompt = pl.program_id(0) < prompt_steps
    if mode == "mix":
        x_ref, ap_ref, as_ref, wo_ref, g_ref, wg_ref, wu_ref, wd_ref, o_ref = refs
        x = x_ref[...] + _dot(jnp.where(in_prompt, ap_ref[...], as_ref[...]), wo_ref[...])
    elif mode == "parts":
        xp_ref, xs_ref, g_ref, wg_ref, wu_ref, wd_ref, o_ref = refs
        x = jnp.where(in_prompt, xp_ref[...], xs_ref[...])
    else:
        x_ref, g_ref, wg_ref, wu_ref, wd_ref, o_ref = refs
        x = x_ref[...]
    xn = _rms(x, g_ref[...]).astype(BF16)
    g = _dot(xn, wg_ref[...])
    u = _dot(xn, wu_ref[...])
    h = (_silu(g) * u).astype(BF16)
    o_ref[...] = x + 0.5 * _dot(h, wd_ref[...])


def _ffn(x, gain, w_gu, w_down, mix=None):
    once = dict(pipeline_mode=pl.Buffered(1))
    parts = isinstance(x, tuple)
    d = (x[0] if parts else x).shape[1]
    t = x[0].shape[0] + x[1].shape[0] if parts else x.shape[0]
    row = pl.BlockSpec((ROW_TILE, d), lambda i: (i, 0))
    prompt_steps = (x[0] if parts else mix[0] if mix is not None else x).shape[0] // ROW_TILE
    two_parts = [pl.BlockSpec((ROW_TILE, d), lambda i: (jnp.minimum(i, prompt_steps - 1), 0)),
                 pl.BlockSpec((ROW_TILE, d), lambda i: (jnp.maximum(i - prompt_steps, 0), 0))]
    if parts:
        mode, in_specs, args = "parts", two_parts, list(x)
    elif mix is not None:
        mode, in_specs, args = "mix", [row] + two_parts + [pl.BlockSpec((d, d), lambda i: (0, 0), **once)], [x, *mix]
    else:
        mode, in_specs, args = "plain", [row], [x]
    in_specs += [
        pl.BlockSpec((1, d), lambda i: (0, 0)),
        pl.BlockSpec((d, D_FF), lambda i: (0, 0), **once),
        pl.BlockSpec((d, D_FF), lambda i: (0, 1), **once),
        pl.BlockSpec((D_FF, d), lambda i: (0, 0), **once),
    ]
    return pl.pallas_call(
        functools.partial(_ffn_kernel, mode=mode, prompt_steps=prompt_steps),
        out_shape=jax.ShapeDtypeStruct((t, d), F32),
        grid=(t // ROW_TILE,),
        in_specs=in_specs,
        out_specs=row,
        compiler_params=_params("parallel"),
        name="ffn",
    )(*args, gain.reshape(1, d), w_gu, w_gu, w_down)


def _pair_head_norm(x, gain2):
    low = _iota2(x.shape, 1) < BAND_DH
    sq = x * x
    s_lo = jnp.sum(jnp.where(low, sq, 0.0), axis=-1, keepdims=True)
    s_hi = jnp.sum(jnp.where(low, 0.0, sq), axis=-1, keepdims=True)
    ms = jnp.where(low, s_lo, s_hi) * (1.0 / BAND_DH)
    return x * lax.rsqrt(ms + EPS) * gain2


def _proj_kernel(*refs, segs, n_gain):
    x_ref, g_ref, w_ref = refs[:3]
    gains = refs[3:3 + n_gain]
    outs = refs[3 + n_gain:]
    xn = _rms(x_ref[...], g_ref[...]).astype(BF16)
    for c0, c1, gain_idx, dsts in segs:
        y = _dot(xn, w_ref[:, c0:c1])
        if gain_idx is not None:
            y = jnp.concatenate([_pair_head_norm(y[:, s:s + LANES], gains[gain_idx][...])
                                 for s in range(0, c1 - c0, LANES)], axis=1)
        for out_idx, off, scale in dsts:
            o_ref = outs[out_idx]
            o_ref[:, off:off + c1 - c0] = (y if scale == 1.0 else y * scale).astype(o_ref.dtype)


def _proj(x, gain, w, out_defs, segs, gains=(), rows=None):
    d = x.shape[1]
    r0, r1 = rows or (0, x.shape[0])
    first = r0 // ROW_TILE
    const = lambda shape: pl.BlockSpec(shape, lambda i: (0, 0))
    return pl.pallas_call(
        functools.partial(_proj_kernel, segs=tuple(segs), n_gain=len(gains)),
        out_shape=tuple(jax.ShapeDtypeStruct((r1 - r0, width), dt) for width, dt in out_defs),
        grid=((r1 - r0) // ROW_TILE,),
        in_specs=[pl.BlockSpec((ROW_TILE, d), lambda i: (first + i, 0)), const((1, d)),
                  pl.BlockSpec(w.shape, lambda i: (0, 0), pipeline_mode=pl.Buffered(1))]
        + [const((1, LANES))] * len(gains),
        out_specs=tuple(pl.BlockSpec((ROW_TILE, width), lambda i: (i, 0)) for width, _ in out_defs),
        compiler_params=_params("parallel"),
        name="proj",
    )(x, gain.reshape(1, d), w, *[jnp.tile(g, 2).reshape(1, LANES) for g in gains])


def _col_segs(n, dsts_of):
    return [(c, min(c + D_MODEL, n), None, dsts_of(c)) for c in range(0, n, D_MODEL)]


def _gla_kernel(*refs, has_init, group):
    if has_init:
        (q_ref, k_ref, v_ref, r_ref, gl_ref, wg2_ref, bg_ref, on_ref, s0_ref,
         o_ref, sfin_ref, st_ref) = refs
    else:
        (q_ref, k_ref, v_ref, r_ref, gl_ref, wg2_ref, bg_ref, on_ref,
         o_ref, sfin_ref, st_ref) = refs
    c = pl.program_id(1)
    units = [(g, h) for g in range(group) for h in range(GLA_HEADS)]

    @pl.when(c == 0)
    def _():
        for g, h in units:
            if has_init:
                st_ref[g * GLA_HEADS + h] = s0_ref[g, h].T
            else:
                st_ref[g * GLA_HEADS + h] = jnp.zeros((GLA_DV, GLA_DK), F32)

    incl = _iota2((CHUNK, CHUNK), 0) >= _iota2((CHUNK, CHUNK), 1)
    tri = jnp.where(incl, 1.0, 0.0).astype(BF16)
    q_e, k_e, k_l, d_last = [], [], [], []
    for g in range(group):
        x = _dot(gl_ref[g].astype(BF16), wg2_ref[...]) + bg_ref[...]
        log_a = _log_sigmoid(x) * (1.0 / GLA_TAU)
        b = _tri_dot(tri, log_a)
        b_last = b[CHUNK - 1:CHUNK, :]
        q = q_ref[g] * (GLA_DK ** -0.5)
        k = k_ref[g]
        q_e.append((q * jnp.exp(b)).astype(BF16))
        k_e.append((k * jnp.exp(-b)).astype(BF16))
        k_l.append((k * jnp.exp(b_last - b)).astype(BF16))
        d_last.append(jnp.exp(b_last))
    gain = on_ref[...]
    ks = [slice(h * GLA_DK, (h + 1) * GLA_DK) for h in range(GLA_HEADS)]
    vs = [slice(h * GLA_DV, (h + 1) * GLA_DV) for h in range(GLA_HEADS)]
    v = [v_ref[g, :, vs[h]].astype(BF16) for g, h in units]
    st = [st_ref[g * GLA_HEADS + h] for g, h in units]
    att = [jnp.where(incl, _dot_nt(q_e[g][:, ks[h]], k_e[g][:, ks[h]]), 0.0).astype(BF16) for g, h in units]
    inter = [_dot_nt(q_e[g][:, ks[h]], st[u].astype(BF16)) for u, (g, h) in enumerate(units)]
    o = [_dot(att[u], v[u]) + inter[u] for u in range(len(units))]
    upd = [_dot_tn(v[u], k_l[g][:, ks[h]]) for u, (g, h) in enumerate(units)]
    for u, (g, h) in enumerate(units):
        st_ref[g * GLA_HEADS + h] = st[u] * d_last[g][:, ks[h]] + upd[u]
        o_ref[g, :, vs[h]] = (_rms(o[u], gain) * _silu(r_ref[g, :, vs[h]])).astype(BF16)

    @pl.when(c == pl.num_programs(1) - 1)
    def _():
        for g, h in units:
            sfin_ref[g, h] = st_ref[g * GLA_HEADS + h].T


def _gla(proj, w_gate2, b_gate, onorm, s0, nb, nc):
    qk = GLA_HEADS * GLA_DK
    vv = GLA_HEADS * GLA_DV
    proj = proj.reshape(nb, nc * CHUNK, proj.shape[1])
    blk = lambda width, col: pl.BlockSpec((GROUP, CHUNK, width), lambda g, c: (g, c, col))
    in_specs = [
        blk(qk, 0), blk(qk, 1), blk(vv, 1), blk(vv, 2), blk(LANES, (2 * qk + 2 * vv) // LANES),
        pl.BlockSpec((LANES, qk), lambda g, c: (0, 0)),
        pl.BlockSpec((1, qk), lambda g, c: (0, 0)),
        pl.BlockSpec((1, GLA_DV), lambda g, c: (0, 0)),
    ]
    args = [proj, proj, proj, proj, proj, w_gate2, b_gate, onorm]
    state = pl.BlockSpec((GROUP, GLA_HEADS, GLA_DK, GLA_DV), lambda g, c: (g, 0, 0, 0))
    if s0 is not None:
        in_specs.append(state)
        args.append(s0)
    o, s_fin = pl.pallas_call(
        functools.partial(_gla_kernel, has_init=s0 is not None, group=GROUP),
        out_shape=(jax.ShapeDtypeStruct((nb, nc * CHUNK, vv), BF16),
                   jax.ShapeDtypeStruct((nb, GLA_HEADS, GLA_DK, GLA_DV), F32)),
        grid=(nb // GROUP, nc),
        in_specs=in_specs,
        out_specs=(blk(vv, 0), state),
        scratch_shapes=[pltpu.VMEM((GROUP * GLA_HEADS, GLA_DV, GLA_DK), F32)],
        compiler_params=_params("parallel", "arbitrary"),
        name="gla",
    )(*args)
    return o.reshape(nb * nc * CHUNK, vv), s_fin


BIAS_WIDTH = 640


def _band_bias_kernel(w_ref, o_ref):
    for h in range(BAND_HEADS):
        x = jnp.broadcast_to(w_ref[h:h + 1, :], (CHUNK, BIAS_WIDTH))
        rolled = pltpu.roll(x, 0, 1, stride=1, stride_axis=0)
        o_ref[h // 2, (h % 2) * CHUNK:(h % 2 + 1) * CHUNK, :] = rolled[:, 0:BAND_SPAN]


def _band_bias(rel_bias):
    c = jnp.arange(BIAS_WIDTH)
    n = jnp.where(c <= BAND_SPAN, c, c - BIAS_WIDTH)
    w = rel_bias[:, jnp.clip(BAND_WINDOW - n, -REL_CLIP, REL_CLIP) + REL_CLIP]
    return pl.pallas_call(
        _band_bias_kernel,
        out_shape=jax.ShapeDtypeStruct((BAND_HEADS // 2, 2 * CHUNK, BAND_SPAN), F32),
        name="band_bias",
    )(w)


def _band_attn_kernel(q_ref, kp_ref, kc_ref, vp_ref, vc_ref, bias_ref, o_ref, kall_ref, vall_ref,
                      *, nq, prompt):
    if prompt:
        kall_ref[0:BAND_WINDOW] = kp_ref[...]
        vall_ref[0:BAND_WINDOW] = vp_ref[...]
    else:
        kall_ref[0:BAND_WINDOW] = kp_ref[0].astype(BF16)
        vall_ref[0:BAND_WINDOW] = vp_ref[0].astype(BF16)
    kall_ref[BAND_WINDOW:] = kc_ref[...]
    vall_ref[BAND_WINDOW:] = vc_ref[...]
    first_pos = (pl.program_id(1) - 1) * BAND_WINDOW if prompt else 0
    key_idx = _iota2((2 * CHUNK, BAND_SPAN), 1)
    pairs = range(D_MODEL // LANES)
    lanes = [slice(p * LANES, (p + 1) * LANES) for p in pairs]

    def chunk(t, carry):
        r0 = pl.multiple_of(t * CHUNK, CHUNK)
        qs = [_stack_pair(q_ref[pl.ds(r0, CHUNK), ls]) for ls in lanes]
        s = [_dot_nt(qs[p], kall_ref[pl.ds(r0, BAND_SPAN), lanes[p]]) + bias_ref[p]
             for p in pairs]
        if prompt:
            visible = first_pos + r0 + key_idx >= 0
            s = [jnp.where(visible, x, -jnp.inf) for x in s]
        e = [jnp.exp(x - jnp.max(x, axis=-1, keepdims=True)) for x in s]
        prob = [(x / jnp.sum(x, axis=-1, keepdims=True)).astype(BF16) for x in e]
        pv = [_dot(prob[p], vall_ref[pl.ds(r0, BAND_SPAN), lanes[p]]) for p in pairs]
        for p in pairs:
            o_ref[pl.ds(r0, CHUNK), lanes[p]] = _unstack_pair(pv[p]).astype(BF16)
        return carry

    lax.fori_loop(0, nq, chunk, 0)


def _band_attn_prompt(qn, knb, vb, bias, nb, seq):
    d = D_MODEL
    blk = BAND_WINDOW
    nblk = seq // blk
    cur = pl.BlockSpec((blk, d), lambda b, i: (b * nblk + i, 0))
    prev = pl.BlockSpec((blk, d), lambda b, i: (b * nblk + jnp.maximum(i - 1, 0), 0))
    return pl.pallas_call(
        functools.partial(_band_attn_kernel, nq=blk // CHUNK, prompt=True),
        out_shape=jax.ShapeDtypeStruct((nb * seq, d), BF16),
        grid=(nb, nblk),
        in_specs=[cur, prev, cur, prev, cur,
                  pl.BlockSpec(bias.shape, lambda b, i: (0, 0, 0))],
        out_specs=cur,
        scratch_shapes=[pltpu.VMEM((2 * blk, d), BF16), pltpu.VMEM((2 * blk, d), BF16)],
        compiler_params=_params("parallel", "arbitrary"),
        name="band_attn_prompt",
    )(qn, knb, knb, vb, vb, bias)


def _band_attn_sample(qn, knb, vb, cache_k, cache_v, bias, nb, row_off):
    d = D_MODEL
    new = pl.BlockSpec((CHUNK, d), lambda b: (row_off + b, 0))
    old = pl.BlockSpec((1, BAND_WINDOW, d), lambda b: (b, 0, 0))
    return pl.pallas_call(
        functools.partial(_band_attn_kernel, nq=1, prompt=False),
        out_shape=jax.ShapeDtypeStruct((nb * CHUNK, d), BF16),
        grid=(nb,),
        in_specs=[new, old, new, old, new, pl.BlockSpec(bias.shape, lambda b: (0, 0, 0))],
        out_specs=pl.BlockSpec((CHUNK, d), lambda b: (b, 0)),
        scratch_shapes=[pltpu.VMEM((BAND_SPAN, d), BF16), pltpu.VMEM((BAND_SPAN, d), BF16)],
        compiler_params=_params("parallel"),
        name="band_attn_sample",
    )(qn, cache_k, knb, cache_v, vb, bias)


def _gdn_prep_kernel(*refs, has_init):
    if has_init:
        x_ref, raw_ref, cw_ref, par_ref, c0_ref, q_ref, k_ref, v_ref, bg_ref, carry_ref = refs
    else:
        x_ref, raw_ref, cw_ref, par_ref, q_ref, k_ref, v_ref, bg_ref, carry_ref = refs
    tb = x_ref.shape[0]

    @pl.when(pl.program_id(1) == 0)
    def _():
        if has_init:
            carry_ref[...] = c0_ref[0]
        else:
            carry_ref[...] = jnp.zeros_like(carry_ref)

    x = x_ref[...]
    w = cw_ref[...]
    x8 = x[0:SUBLANES]
    c8 = carry_ref[...]
    row8 = _iota2(x8.shape, 0)
    y = x * w[CONV_W - 1:CONV_W]
    y8 = x8 * w[CONV_W - 1:CONV_W]
    for s in range(1, CONV_W):
        ws = w[CONV_W - 1 - s:CONV_W - s]
        y = y + pltpu.roll(x, s, 0) * ws
        y8 = y8 + jnp.where(row8 < s, pltpu.roll(c8, s, 0), pltpu.roll(x8, s, 0)) * ws
    carry_ref[...] = x[tb - SUBLANES:tb]
    if tb > SUBLANES:
        y = jnp.concatenate([y8, y[SUBLANES:]], axis=0)
    else:
        y = y8
    y = _silu(y)
    qk = GDN_HEADS * GDN_DK
    for h in range(GDN_HEADS):
        qh = y[:, h * GDN_DK:(h + 1) * GDN_DK]
        kh = y[:, qk + h * GDN_DK:qk + (h + 1) * GDN_DK]
        q_ref[h] = qh * lax.rsqrt(jnp.sum(qh * qh, axis=-1, keepdims=True) + EPS)
        k_ref[h] = kh * lax.rsqrt(jnp.sum(kh * kh, axis=-1, keepdims=True) + EPS)
        v_ref[h] = y[:, 2 * qk + h * GDN_DV:2 * qk + (h + 1) * GDN_DV]
    raw = raw_ref[...]
    beta = _sigmoid(raw)
    g = par_ref[0:1, :] * _softplus(raw + par_ref[1:2, :])
    bg_ref[...] = jnp.where(_iota2(raw.shape, 1) < GDN_HEADS, beta, g)


def _gdn_prep(proj, conv_w, par, c0, nb, seq, row_off, tb):
    nblk = seq // tb
    base = row_off // tb
    row = lambda b, i: base + b * nblk + i
    in_specs = [
        pl.BlockSpec((tb, GDN_CONV_CH), lambda b, i: (row(b, i), 0)),
        pl.BlockSpec((tb, LANES), lambda b, i: (row(b, i), (GDN_CONV_CH + GDN_HEADS * GDN_DV) // LANES)),
        pl.BlockSpec((CONV_W, GDN_CONV_CH), lambda b, i: (0, 0)),
        pl.BlockSpec((2, LANES), lambda b, i: (0, 0)),
    ]
    args = [proj, proj, conv_w, par]
    if c0 is not None:
        in_specs.append(pl.BlockSpec((1, SUBLANES, GDN_CONV_CH), lambda b, i: (b, 0, 0)))
        args.append(c0)
    n = nb * seq
    head = pl.BlockSpec((GDN_HEADS, tb, GDN_DK), lambda b, i: (0, b * nblk + i, 0))
    return pl.pallas_call(
        functools.partial(_gdn_prep_kernel, has_init=c0 is not None),
        out_shape=(jax.ShapeDtypeStruct((GDN_HEADS, n, GDN_DK), F32),
                   jax.ShapeDtypeStruct((GDN_HEADS, n, GDN_DK), F32),
                   jax.ShapeDtypeStruct((GDN_HEADS, n, GDN_DV), F32),
                   jax.ShapeDtypeStruct((n, LANES), F32)),
        grid=(nb, nblk),
        in_specs=in_specs,
        out_specs=(head, head, head, pl.BlockSpec((tb, LANES), lambda b, i: (b * nblk + i, 0))),
        scratch_shapes=[pltpu.VMEM((SUBLANES, GDN_CONV_CH), F32)],
        compiler_params=_params("parallel", "arbitrary"),
        name="gdn_prep",
    )(*args)


def _gdn_kernel(*refs, has_init, group):
    if has_init:
        q_ref, k_ref, v_ref, bg_ref, gate_ref, on_ref, s0_ref, o_ref, sfin_ref, s_ref = refs
    else:
        q_ref, k_ref, v_ref, bg_ref, gate_ref, on_ref, o_ref, sfin_ref, s_ref = refs
    c = pl.program_id(1)

    @pl.when(c == 0)
    def _():
        if has_init:
            s_ref[...] = s0_ref[...].reshape(s_ref.shape)
        else:
            s_ref[...] = jnp.zeros_like(s_ref)

    ri = _iota2((CHUNK, CHUNK), 0)
    ci = _iota2((CHUNK, CHUNK), 1)
    incl = ri >= ci
    strict = ri > ci
    tri = jnp.where(incl, 1.0, 0.0).astype(BF16)
    eye = jnp.where(ri == ci, 1.0, 0.0)
    gain = on_ref[...]
    units = [(g, h) for g in range(group) for h in range(GDN_HEADS)]
    heads = range(len(units))
    bg = [bg_ref[g] for g in range(group)]
    cs = [_tri_dot(tri, x) for x in bg]
    cs_t = [x.T for x in cs]
    beta = [bg[g][:, h:h + 1] for g, h in units]
    gcol = [cs[g][:, GDN_HEADS + h:GDN_HEADS + h + 1] for g, h in units]
    glast = [x[CHUNK - 1:CHUNK, :] for x in gcol]
    gam = [jnp.exp(jnp.where(incl, gcol[u] - cs_t[g][GDN_HEADS + h:GDN_HEADS + h + 1, :], -jnp.inf))
           for u, (g, h) in enumerate(units)]
    egc = [jnp.exp(x) for x in gcol]
    k = [k_ref[h, g] for g, h in units]
    k16 = [x.astype(BF16) for x in k]
    kb = [k[h] * beta[h] for h in heads]
    m = [jnp.where(strict, _dot_nt(kb[h].astype(BF16), k16[h]) * gam[h], 0.0) for h in heads]
    d = [eye - jnp.where((ri == ci + 1) & ((ci & 1) == 0), m[h], 0.0) for h in heads]
    for lvl in range(1, 6):
        sub = (((ri >> lvl) & 1) == 1) & (((ci >> lvl) & 1) == 0) & ((ri >> (lvl + 1)) == (ci >> (lvl + 1)))
        y = [_dot_f32(d[h], jnp.where(sub, m[h], 0.0)) for h in heads]
        x = [_dot_f32(y[h], d[h]) for h in heads]
        d = [d[h] - x[h] for h in heads]
    t16 = [t.astype(BF16) for t in d]
    u = [_dot(t16[i], (v_ref[h, g] * beta[i]).astype(BF16)) for i, (g, h) in enumerate(units)]
    w = [_dot(t16[h], (kb[h] * egc[h]).astype(BF16)) for h in heads]
    q = [q_ref[h, g] * (GDN_DK ** -0.5) for g, h in units]
    att = [_dot_nt(q[h].astype(BF16), k16[h]) * gam[h] for h in heads]
    q_e = [(q[h] * egc[h]).astype(BF16) for h in heads]
    k_l = [(k[h] * jnp.exp(glast[h] - gcol[h])).astype(BF16) for h in heads]
    s = [s_ref[h] for h in heads]
    s16 = [x.astype(BF16) for x in s]
    v_new = [u[h] - _dot(w[h].astype(BF16), s16[h]) for h in heads]
    v16 = [x.astype(BF16) for x in v_new]
    o = [_dot(q_e[h], s16[h]) + _dot(att[h].astype(BF16), v16[h]) for h in heads]
    for i, (g, h) in enumerate(units):
        s_ref[i] = s[i] * jnp.exp(glast[i]) + _dot_tn(k_l[i], v16[i])
        vs = slice(h * GDN_DV, (h + 1) * GDN_DV)
        o_ref[g, :, vs] = (_rms(o[i], gain) * _silu(gate_ref[g, :, vs])).astype(BF16)

    @pl.when(c == pl.num_programs(1) - 1)
    def _():
        sfin_ref[...] = s_ref[...].reshape(sfin_ref.shape)


def _gdn(qh, kh, vh, bg, proj, onorm, s0, nb, nc):
    vv = GDN_HEADS * GDN_DV
    seq = nc * CHUNK
    qh, kh, vh = (t.reshape(GDN_HEADS, nb, seq, t.shape[-1]) for t in (qh, kh, vh))
    bg = bg.reshape(nb, seq, LANES)
    proj = proj.reshape(nb, seq, proj.shape[1])
    head = pl.BlockSpec((GDN_HEADS, GROUP, CHUNK, GDN_DK), lambda g, c: (0, g, c, 0))
    in_specs = [
        head, head, head,
        pl.BlockSpec((GROUP, CHUNK, LANES), lambda g, c: (g, c, 0)),
        pl.BlockSpec((GROUP, CHUNK, vv), lambda g, c: (g, c, GDN_CONV_CH // vv)),
        pl.BlockSpec((1, GDN_DV), lambda g, c: (0, 0)),
    ]
    args = [qh, kh, vh, bg, proj, onorm]
    state = pl.BlockSpec((GROUP, GDN_HEADS, GDN_DK, GDN_DV), lambda g, c: (g, 0, 0, 0))
    if s0 is not None:
        in_specs.append(state)
        args.append(s0)
    o, s_fin = pl.pallas_call(
        functools.partial(_gdn_kernel, has_init=s0 is not None, group=GROUP),
        out_shape=(jax.ShapeDtypeStruct((nb, seq, vv), BF16),
                   jax.ShapeDtypeStruct((nb, GDN_HEADS, GDN_DK, GDN_DV), F32)),
        grid=(nb // GROUP, nc),
        in_specs=in_specs,
        out_specs=(pl.BlockSpec((GROUP, CHUNK, vv), lambda g, c: (g, c, 0)), state),
        scratch_shapes=[pltpu.VMEM((GROUP * GDN_HEADS, GDN_DK, GDN_DV), F32)],
        compiler_params=_params("parallel", "arbitrary"),
        name="gdn",
    )(*args)
    return o.reshape(nb * seq, vv), s_fin


def _neg_abs(x):
    bits = lax.bitcast_convert_type(x, jnp.uint32) | jnp.uint32(0x80000000)
    return lax.bitcast_convert_type(bits, F32)


def _stack_pair(q2):
    low = _iota2(q2.shape, 1) < SB_DH
    return jnp.concatenate([jnp.where(low, q2, 0), jnp.where(low, 0, q2)], axis=0)


def _unstack_pair(acc):
    tq = acc.shape[0] // 2
    low = _iota2((tq, LANES), 1) < SB_DH
    return jnp.where(low, acc[:tq], acc[tq:])


def _upper(tk):
    return jnp.where(_iota2((tk, tk), 0) > _iota2((tk, tk), 1), 1.0, 0.0).astype(BF16)


def _diag_mask(tq, tk):
    r = _iota2((2 * tq, tk), 0)
    r = jnp.where(r >= tq, r - tq, r)
    return _iota2((2 * tq, tk), 1) < r


SB_NEUTRAL = 1e30
SB_LAG = 4
SB_CHAINS = 1


def _sb_log_terms(zn, mask):
    l1m = jnp.minimum(zn, 0.0) - jnp.log2(1.0 + jnp.exp2(_neg_abs(zn)))
    log_b = l1m - zn
    if mask is not None:
        l1m = jnp.where(mask, l1m, 0.0)
        log_b = jnp.where(mask, log_b, -SB_NEUTRAL)
    return l1m, log_b


def _sb_direct(qs, k2, v2, mask):
    l1m, log_b = _sb_log_terms(_dot_nt(qs, k2), mask)
    a = jnp.exp2(log_b + _dot(l1m.astype(BF16), _upper(k2.shape[0])))
    return jnp.sum(l1m, axis=-1, keepdims=True), _dot(a.astype(BF16), v2)


def _sb_pipeline(nhalf, nchain, q_of, k_of, v_of, st_of, upper, scratch, mask):
    z_ref, hi_ref, logb_ref, tot_ref, cs_ref, a_ref, run_ref, acc_ref = scratch
    logb_ref[...] = jnp.full(logb_ref.shape, -SB_NEUTRAL, F32)
    tot_ref[...] = jnp.zeros(tot_ref.shape, F32)
    for c in range(nchain):
        z_ref[2 * c + 1] = jnp.full(z_ref.shape[1:], SB_NEUTRAL, F32)
        hi_ref[2 * c + 1] = jnp.zeros(hi_ref.shape[1:], BF16)
        cs_ref[2 * c + 1] = jnp.zeros(cs_ref.shape[1:], F32)
        a_ref[2 * c + 1] = jnp.zeros(a_ref.shape[1:], BF16)

    def half(h, p):
        for c in range(nchain):
            rows = st_of(h - 4, c)
            acc_ref[rows, :] += _dot(a_ref[2 * c + 1 - p], v_of(h - 4, c))
        for c in range(nchain):
            rows = st_of(h - 3, c)
            run = run_ref[rows, :]
            a_ref[2 * c + p] = jnp.exp2(cs_ref[2 * c + 1 - p] + (logb_ref[2 * c + p] + run)).astype(BF16)
            run_ref[rows, :] = run + tot_ref[2 * c + p]
        for c in range(nchain):
            cs_ref[2 * c + p] = _dot(hi_ref[2 * c + 1 - p], upper)
        for c in range(nchain):
            l1m, logb_ref[2 * c + p] = _sb_log_terms(z_ref[2 * c + 1 - p], mask)
            hi_ref[2 * c + p] = l1m.astype(BF16)
            tot_ref[2 * c + p] = jnp.sum(l1m, axis=-1, keepdims=True)
        for c in range(nchain):
            z_ref[2 * c + p] = _dot_nt(q_of(h, c), k_of(h, c))

    def body(i, _):
        half(2 * i, 0)
        half(2 * i + 1, 1)
        return 0

    assert nhalf % 2 == 0
    lax.fori_loop(0, nhalf // 2, body, 0)


def _sb_rings(nchain, rows, tk):
    slots = 2 * nchain
    return [pltpu.VMEM((slots, rows, tk), F32), pltpu.VMEM((slots, rows, tk), BF16),
            pltpu.VMEM((slots, rows, tk), F32), pltpu.VMEM((slots, rows, 1), F32),
            pltpu.VMEM((slots, rows, tk), F32), pltpu.VMEM((slots, rows, tk), BF16)]


def _sb_prompt_items(nq):
    diag = [(i, i) for i in range(nq)]
    off = [(j, i) for j in range(nq - 2, -1, -1) for i in range(j + 1, nq)]
    pad = [(0, nq)] * (SB_CHAINS * SB_LAG)
    assert len(diag) % (2 * SB_CHAINS) == 0 and len(off) % (2 * SB_CHAINS) == 0
    rows = []
    for items in (diag, off):
        padded = pad + items + pad
        rows += [[j for j, _ in padded], [i for _, i in padded]]
    width = max(len(r) for r in rows)
    tab = np.zeros((4, width), np.int32)
    for r, row in enumerate(rows):
        tab[r, :len(row)] = row
    return jnp.asarray(tab), len(diag), len(off)


def _sb_prompt_kernel(tab_ref, q_ref, k_ref, v_ref, o_ref, qs_ref, run_ref, acc_ref, *rings, nq, ndiag, noff):
    tk = SB_TK
    rows = 2 * tk

    def stack(i, _):
        qs_ref[pl.ds(pl.multiple_of(i * rows, rows), rows), :] = _stack_pair(
            q_ref[pl.ds(pl.multiple_of(i * tk, tk), tk), :])
        return 0

    lax.fori_loop(0, nq, stack, 0)
    qs_ref[nq * rows:(nq + 1) * rows, :] = jnp.zeros((rows, LANES), BF16)
    run_ref[...] = jnp.zeros(run_ref.shape, F32)
    acc_ref[...] = jnp.zeros(acc_ref.shape, F32)
    upper = _upper(tk)
    scratch = list(rings) + [run_ref, acc_ref]
    for row, nitems, mask in ((0, ndiag, _diag_mask(tk, tk)), (2, noff, None)):
        col = lambda hh, c: SB_CHAINS * (hh + SB_LAG) + c
        q_rows = lambda hh, c: pl.ds(pl.multiple_of(tab_ref[row + 1, col(hh, c)] * rows, rows), rows)
        k_rows = lambda hh, c: pl.ds(pl.multiple_of(tab_ref[row, col(hh, c)] * tk, tk), tk)
        _sb_pipeline(nitems // SB_CHAINS + SB_LAG, SB_CHAINS,
                     lambda hh, c: qs_ref[q_rows(hh, c), :],
                     lambda hh, c: k_ref[k_rows(hh, c), :],
                     lambda hh, c: v_ref[k_rows(hh, c), :],
                     q_rows, upper, scratch, mask)

    def emit(i, _):
        o_ref[pl.ds(pl.multiple_of(i * tk, tk), tk), :] = _unstack_pair(
            acc_ref[pl.ds(pl.multiple_of(i * rows, rows), rows), :]).astype(BF16)
        return 0

    lax.fori_loop(0, nq, emit, 0)


def _sb_prompt(qb, kb, vb, nb, seq):
    d = D_MODEL
    npair = d // LANES
    nq = seq // SB_TK
    rows = 2 * SB_TK
    tab, ndiag, noff = _sb_prompt_items(nq)
    col = pl.BlockSpec((seq, LANES), lambda b, p, tab: (b, p))
    return pl.pallas_call(
        functools.partial(_sb_prompt_kernel, nq=nq, ndiag=ndiag, noff=noff),
        out_shape=jax.ShapeDtypeStruct((nb * seq, d), BF16),
        grid_spec=pltpu.PrefetchScalarGridSpec(
            num_scalar_prefetch=1,
            grid=(nb, npair),
            in_specs=[col, col, col],
            out_specs=col,
            scratch_shapes=[pltpu.VMEM(((nq + 1) * rows, LANES), BF16), pltpu.VMEM(((nq + 1) * rows, 1), F32),
                            pltpu.VMEM(((nq + 1) * rows, LANES), F32)] + _sb_rings(SB_CHAINS, rows, SB_TK)),
        compiler_params=_params("parallel", "arbitrary"),
        name="sb_prompt",
    )(tab, qb, kb, vb)


def _sb_sample_kernel(q_ref, kn_ref, vn_ref, kc_ref, vc_ref, o_ref, run_ref, acc_ref, *rings):
    tq, tk = CHUNK, SB_TK
    rows = 2 * tq
    nblk = kc_ref.shape[1] // tk
    lanes = lambda c: slice(c * LANES, (c + 1) * LANES)
    qs = [_stack_pair(q_ref[:, lanes(c)]) for c in range(2)]
    for c in range(2):
        run, acc = _sb_direct(qs[c], kn_ref[:, lanes(c)], vn_ref[:, lanes(c)], _diag_mask(tq, tq))
        run_ref[c * rows:(c + 1) * rows, :] = run
        acc_ref[c * rows:(c + 1) * rows, :] = acc

    def k_rows(hh):
        return pl.ds(pl.multiple_of(jnp.clip(nblk - 1 - hh, 0, nblk - 1) * tk, tk), tk)

    def st_of(hh, c):
        real = (hh >= 0) & (hh < nblk)
        return pl.ds(pl.multiple_of(jnp.where(real, c, 2) * rows, rows), rows)

    _sb_pipeline(nblk + SB_LAG, 2,
                 lambda hh, c: qs[c],
                 lambda hh, c: kc_ref[0, k_rows(hh), lanes(c)].astype(BF16),
                 lambda hh, c: vc_ref[0, k_rows(hh), lanes(c)].astype(BF16),
                 st_of, _upper(tk), list(rings) + [run_ref, acc_ref], None)
    for c in range(2):
        o_ref[:, lanes(c)] = _unstack_pair(acc_ref[c * rows:(c + 1) * rows, :]).astype(BF16)


def _sb_sample(qb, kb, vb, cache_k, cache_v, nb, row_off):
    d = D_MODEL
    wide = 2 * LANES
    ngrp = d // wide
    past = cache_k.shape[1]
    rows = 2 * CHUNK
    new = pl.BlockSpec((CHUNK, wide), lambda b, g: (row_off + b, g))
    old = pl.BlockSpec((1, past, wide), lambda b, g: (b, 0, g))
    return pl.pallas_call(
        _sb_sample_kernel,
        out_shape=jax.ShapeDtypeStruct((nb * CHUNK, d), BF16),
        grid=(nb, ngrp),
        in_specs=[new, new, new, old, old],
        out_specs=pl.BlockSpec((CHUNK, wide), lambda b, g: (b, g)),
        scratch_shapes=[pltpu.VMEM((3 * rows, 1), F32), pltpu.VMEM((3 * rows, LANES), F32)]
        + _sb_rings(2, rows, SB_TK),
        compiler_params=_params("parallel", "parallel"),
        name="sb_sample",
    )(qb, kb, vb, cache_k, cache_v)


def _pad_cols(w, n):
    return jnp.pad(w, ((0, 0), (0, n - w.shape[1])))


def kernel(x_prompt, x_sample, state_gla, cache_band_k, cache_band_v, state_gdn, state_gdn_conv,
           cache_sb_k, cache_sb_v, ffn1_norm, ffn1_w_gu, ffn1_w_down, mix_norm, ffn2_norm,
           ffn2_w_gu, ffn2_w_down, gla_w_in, gla_w_gate2, gla_b_gate, gla_onorm, gla_w_out,
           band_w_in, band_q_norm, band_k_norm, band_rel_bias, band_w_out, gdn_w_in, gdn_conv_w,
           gdn_a_log, gdn_dt_bias, gdn_onorm, gdn_w_out, sb_w_in, sb_w_out):
    bp, seq, d = x_prompt.shape
    bs, dec = x_sample.shape[:2]
    assert dec == CHUNK and seq % BAND_WINDOW == 0 and d == D_MODEL
    n_p = bp * seq
    n_s = bs * dec
    nc_p = seq // CHUNK
    x = (x_prompt.reshape(n_p, d), x_sample.reshape(n_s, d))
    depth = ffn1_norm.shape[0]
    outs = {}
    for i in range(depth):
        m, j = i % 4, i // 4
        x = _ffn(x, ffn1_norm[i], ffn1_w_gu[i].astype(BF16), ffn1_w_down[i].astype(BF16))
        if m == 0:
            n_in = 2 * GLA_HEADS * GLA_DK + 2 * GLA_HEADS * GLA_DV + LANES
            gla_proj = functools.partial(_proj, x, mix_norm[i], _pad_cols(gla_w_in[j], n_in).astype(BF16),
                                         [(n_in, F32)], _col_segs(n_in, lambda c: [(0, c, 1.0)]))
            wg2 = jnp.pad(gla_w_gate2[j], ((0, LANES - GLA_RANK), (0, 0))).astype(BF16)
            common = (wg2, gla_b_gate[j].reshape(1, -1), gla_onorm[j].reshape(1, -1))
            o_p, s_p = _gla(*gla_proj(rows=(0, n_p)), *common, None, bp, nc_p)
            o_s, s_s = _gla(*gla_proj(rows=(n_p, n_p + n_s)), *common, state_gla[j], bs, 1)
            outs.setdefault("gla", []).append((s_p, s_s))
            w_out = gla_w_out[j]
        elif m == 1:
            band_proj = functools.partial(
                _proj, x, mix_norm[i], band_w_in[j].astype(BF16),
                [(d, BF16), (d, F32), (d, BF16), (d, F32), (d, BF16)],
                [(0, d, 0, [(0, 0, BAND_DH ** -0.5)]), (d, 2 * d, 1, [(1, 0, 1.0), (2, 0, 1.0)]),
                 (2 * d, 3 * d, None, [(3, 0, 1.0), (4, 0, 1.0)])],
                gains=(band_q_norm[j], band_k_norm[j]))
            qn_p, kn_p, knb_p, v_p, vb_p = band_proj(rows=(0, n_p))
            qn_s, kn_s, knb_s, v_s, vb_s = band_proj(rows=(n_p, n_p + n_s))
            bias = _band_bias(band_rel_bias[j])
            o_p = _band_attn_prompt(qn_p, knb_p, vb_p, bias, bp, seq)
            o_s = _band_attn_sample(qn_s, knb_s, vb_s, cache_band_k[j].reshape(bs, BAND_WINDOW, d),
                                    cache_band_v[j].reshape(bs, BAND_WINDOW, d), bias, bs, 0)
            keep = min(BAND_WINDOW, seq)
            shp = (BAND_HEADS, BAND_DH)
            tail = lambda a: jnp.stack([a[(b + 1) * seq - keep:(b + 1) * seq] for b in range(bp)]).reshape(bp, keep, *shp)
            outs.setdefault("band", []).append((
                tail(kn_p), tail(v_p), kn_s.reshape(bs, dec, *shp), v_s.reshape(bs, dec, *shp)))
            w_out = band_w_out[j]
        elif m == 2:
            n_in = GDN_CONV_CH + GDN_HEADS * GDN_DV + LANES
            gdn_proj = functools.partial(_proj, x, mix_norm[i], _pad_cols(gdn_w_in[j], n_in).astype(BF16),
                                         [(n_in, F32)], _col_segs(n_in, lambda c: [(0, c, 1.0)]))
            (proj_p,), (proj_s,) = gdn_proj(rows=(0, n_p)), gdn_proj(rows=(n_p, n_p + n_s))
            par = jnp.zeros((2, LANES), F32)
            par = par.at[0, GDN_HEADS:2 * GDN_HEADS].set(-jnp.exp(gdn_a_log[j]))
            par = par.at[1, GDN_HEADS:2 * GDN_HEADS].set(gdn_dt_bias[j])
            c0 = jnp.pad(state_gdn_conv[j], ((0, 0), (SUBLANES - (CONV_W - 1), 0), (0, 0)))
            pre_p = _gdn_prep(proj_p, gdn_conv_w[j], par, None, bp, seq, 0, ROW_TILE)
            pre_s = _gdn_prep(proj_s, gdn_conv_w[j], par, c0, bs, dec, 0, dec)
            onorm = gdn_onorm[j].reshape(1, -1)
            o_p, s_p = _gdn(*pre_p, proj_p, onorm, None, bp, nc_p)
            o_s, s_s = _gdn(*pre_s, proj_s, onorm, state_gdn[j], bs, 1)
            last = CONV_W - 1
            conv_p = jnp.stack([proj_p[(b + 1) * seq - last:(b + 1) * seq, :GDN_CONV_CH] for b in range(bp)])
            conv_s = proj_s[:, :GDN_CONV_CH].reshape(bs, dec, -1)[:, dec - last:]
            outs.setdefault("gdn", []).append((s_p, conv_p, s_s, conv_s))
            w_out = gdn_w_out[j]
        else:
            sb_proj = functools.partial(
                _proj, x, mix_norm[i], sb_w_in[j].astype(BF16),
                [(d, BF16), (d, F32), (d, BF16), (d, F32), (d, BF16)],
                [(0, d, None, [(0, 0, -LOG2E * SB_DH ** -0.5)]), (d, 2 * d, None, [(1, 0, 1.0), (2, 0, 1.0)]),
                 (2 * d, 3 * d, None, [(3, 0, 1.0), (4, 0, 1.0)])])
            qb_p, k_p, kb_p, v_p, vb_p = sb_proj(rows=(0, n_p))
            qb_s, k_s, kb_s, v_s, vb_s = sb_proj(rows=(n_p, n_p + n_s))
            o_p = _sb_prompt(qb_p, kb_p, vb_p, bp, seq)
            o_s = _sb_sample(qb_s, kb_s, vb_s, cache_sb_k[j].reshape(bs, -1, d),
                             cache_sb_v[j].reshape(bs, -1, d), bs, 0)
            shp = (SB_HEADS, SB_DH)
            outs.setdefault("sb", []).append((
                k_p.reshape(bp, seq, *shp), v_p.reshape(bp, seq, *shp),
                k_s.reshape(bs, dec, *shp), v_s.reshape(bs, dec, *shp)))
            w_out = sb_w_out[j]
        x = _ffn(x, ffn2_norm[i], ffn2_w_gu[i].astype(BF16), ffn2_w_down[i].astype(BF16),
                 mix=(o_p, o_s, w_out.astype(BF16)))
    stack = lambda key, idx: jnp.stack([t[idx] for t in outs[key]])
    return (x[:n_p].reshape(bp, seq, d), x[n_p:].reshape(bs, dec, d),
            stack("gla", 0), stack("gla", 1),
            stack("band", 0), stack("band", 1), stack("band", 2), stack("band", 3),
            stack("gdn", 0), stack("gdn", 1), stack("gdn", 2), stack("gdn", 3),
            stack("sb", 0), stack("sb", 1), stack("sb", 2), stack("sb", 3))
```

```python
import functools

import jax
import jax.numpy as jnp
import numpy as np
from jax import lax
from jax.experimental import pallas as pl
from jax.experimental.pallas import tpu as pltpu

F32 = jnp.float32
BF16 = jnp.bfloat16

D_MODEL = 1024
CHUNK = 64
EPS = 1e-6
D_FF = 2816
GLA_HEADS, GLA_DK, GLA_DV, GLA_RANK, GLA_TAU = 4, 128, 256, 16, 16.0
BAND_HEADS, BAND_DH, LEFT_CHUNKS, REL_CLIP = 16, 64, 8, 256
BAND_WINDOW = LEFT_CHUNKS * CHUNK
BAND_SPAN = BAND_WINDOW + CHUNK
GDN_HEADS, GDN_DK, GDN_DV, CONV_W = 8, 128, 128, 4
GDN_CONV_CH = 2 * GDN_HEADS * GDN_DK + GDN_HEADS * GDN_DV
SB_HEADS, SB_DH = 16, 64
LOG2E = 1.4426950408889634

LANES = 128
SUBLANES = 8
VMEM_LIMIT = 56 * 1024 * 1024
ROW_TILE = 512
SB_TK = 256
GROUP = 2


def _params(*sem):
    return pltpu.CompilerParams(dimension_semantics=sem, vmem_limit_bytes=VMEM_LIMIT)


def _dot(a, b):
    return jnp.dot(a, b, preferred_element_type=F32)


def _dot_nt(a, b):
    return lax.dot_general(a, b, (((1,), (1,)), ((), ())), preferred_element_type=F32)


def _dot_tn(a, b):
    return lax.dot_general(a, b, (((0,), (0,)), ((), ())), preferred_element_type=F32)


def _split3(x):
    h1 = x.astype(BF16)
    r1 = x - h1.astype(F32)
    h2 = r1.astype(BF16)
    h3 = (r1 - h2.astype(F32)).astype(BF16)
    return h1, h2, h3


def _tri_dot(tri, x):
    h1, h2, h3 = _split3(x)
    return _dot(tri, h1) + _dot(tri, h2) + _dot(tri, h3)


def _dot_f32(a, b):
    a1 = a.astype(BF16)
    a2 = (a - a1.astype(F32)).astype(BF16)
    b1 = b.astype(BF16)
    b2 = (b - b1.astype(F32)).astype(BF16)
    return _dot(a1, b1) + (_dot(a1, b2) + _dot(a2, b1))


def _sigmoid(x):
    return 1.0 / (1.0 + jnp.exp(-x))


def _silu(x):
    return x * _sigmoid(x)


def _softplus(x):
    return jnp.maximum(x, 0.0) + jnp.log1p(jnp.exp(-jnp.abs(x)))


def _log_sigmoid(x):
    return -_softplus(-x)


def _rms(x, g):
    return x * lax.rsqrt(jnp.mean(x * x, axis=-1, keepdims=True) + EPS) * g


def _iota2(shape, dim):
    return lax.broadcasted_iota(jnp.int32, shape, dim)


def _ffn_kernel(*refs, mode, prompt_steps):
    in_prompt = pl.program_id(0) < prompt_steps
    if mode == "mix":
        x_ref, ap_ref, as_ref, wo_ref, g_ref, wg_ref, wu_ref, wd_ref, o_ref = refs
        x = x_ref[...] + _dot(jnp.where(in_prompt, ap_ref[...], as_ref[...]), wo_ref[...])
    elif mode == "parts":
        xp_ref, xs_ref, g_ref, wg_ref, wu_ref, wd_ref, o_ref = refs
        x = jnp.where(in_prompt, xp_ref[...], xs_ref[...])
    else:
        x_ref, g_ref, wg_ref, wu_ref, wd_ref, o_ref = refs
        x = x_ref[...]
    xn = _rms(x, g_ref[...]).astype(BF16)
    g = _dot(xn, wg_ref[...])
    u = _dot(xn, wu_ref[...])
    h = (_silu(g) * u).astype(BF16)
    o_ref[...] = x + 0.5 * _dot(h, wd_ref[...])


def _ffn(x, gain, w_gu, w_down, mix=None):
    once = dict(pipeline_mode=pl.Buffered(1))
    parts = isinstance(x, tuple)
    d = (x[0] if parts else x).shape[1]
    t = x[0].shape[0] + x[1].shape[0] if parts else x.shape[0]
    row = pl.BlockSpec((ROW_TILE, d), lambda i: (i, 0))
    prompt_steps = (x[0] if parts else mix[0] if mix is not None else x).shape[0] // ROW_TILE
    two_parts = [pl.BlockSpec((ROW_TILE, d), lambda i: (jnp.minimum(i, prompt_steps - 1), 0)),
                 pl.BlockSpec((ROW_TILE, d), lambda i: (jnp.maximum(i - prompt_steps, 0), 0))]
    if parts:
        mode, in_specs, args = "parts", two_parts, list(x)
    elif mix is not None:
        mode, in_specs, args = "mix", [row] + two_parts + [pl.BlockSpec((d, d), lambda i: (0, 0), **once)], [x, *mix]
    else:
        mode, in_specs, args = "plain", [row], [x]
    in_specs += [
        pl.BlockSpec((1, d), lambda i: (0, 0)),
        pl.BlockSpec((d, D_FF), lambda i: (0, 0), **once),
        pl.BlockSpec((d, D_FF), lambda i: (0, 1), **once),
        pl.BlockSpec((D_FF, d), lambda i: (0, 0), **once),
    ]
    return pl.pallas_call(
        functools.partial(_ffn_kernel, mode=mode, prompt_steps=prompt_steps),
        out_shape=jax.ShapeDtypeStruct((t, d), F32),
        grid=(t // ROW_TILE,),
        in_specs=in_specs,
        out_specs=row,
        compiler_params=_params("parallel"),
        name="ffn",
    )(*args, gain.reshape(1, d), w_gu, w_gu, w_down)


def _pair_head_norm(x, gain2):
    low = _iota2(x.shape, 1) < BAND_DH
    sq = x * x
    s_lo = jnp.sum(jnp.where(low, sq, 0.0), axis=-1, keepdims=True)
    s_hi = jnp.sum(jnp.where(low, 0.0, sq), axis=-1, keepdims=True)
    ms = jnp.where(low, s_lo, s_hi) * (1.0 / BAND_DH)
    return x * lax.rsqrt(ms + EPS) * gain2


def _proj_kernel(*refs, segs, n_gain):
    x_ref, g_ref, w_ref = refs[:3]
    gains = refs[3:3 + n_gain]
    outs = refs[3 + n_gain:]
    xn = _rms(x_ref[...], g_ref[...]).astype(BF16)
    for c0, c1, gain_idx, dsts in segs:
        y = _dot(xn, w_ref[:, c0:c1])
        if gain_idx is not None:
            y = jnp.concatenate([_pair_head_norm(y[:, s:s + LANES], gains[gain_idx][...])
                                 for s in range(0, c1 - c0, LANES)], axis=1)
        for out_idx, off, scale in dsts:
            o_ref = outs[out_idx]
            o_ref[:, off:off + c1 - c0] = (y if scale == 1.0 else y * scale).astype(o_ref.dtype)


def _proj(x, gain, w, out_defs, segs, gains=(), rows=None):
    d = x.shape[1]
    r0, r1 = rows or (0, x.shape[0])
    first = r0 // ROW_TILE
    const = lambda shape: pl.BlockSpec(shape, lambda i: (0, 0))
    return pl.pallas_call(
        functools.partial(_proj_kernel, segs=tuple(segs), n_gain=len(gains)),
        out_shape=tuple(jax.ShapeDtypeStruct((r1 - r0, width), dt) for width, dt in out_defs),
        grid=((r1 - r0) // ROW_TILE,),
        in_specs=[pl.BlockSpec((ROW_TILE, d), lambda i: (first + i, 0)), const((1, d)),
                  pl.BlockSpec(w.shape, lambda i: (0, 0), pipeline_mode=pl.Buffered(1))]
        + [const((1, LANES))] * len(gains),
        out_specs=tuple(pl.BlockSpec((ROW_TILE, width), lambda i: (i, 0)) for width, _ in out_defs),
        compiler_params=_params("parallel"),
        name="proj",
    )(x, gain.reshape(1, d), w, *[jnp.tile(g, 2).reshape(1, LANES) for g in gains])


def _col_segs(n, dsts_of):
    return [(c, min(c + D_MODEL, n), None, dsts_of(c)) for c in range(0, n, D_MODEL)]


def _gla_kernel(*refs, has_init, group):
    if has_init:
        (q_ref, k_ref, v_ref, r_ref, gl_ref, wg2_ref, bg_ref, on_ref, s0_ref,
         o_ref, sfin_ref, st_ref) = refs
    else:
        (q_ref, k_ref, v_ref, r_ref, gl_ref, wg2_ref, bg_ref, on_ref,
         o_ref, sfin_ref, st_ref) = refs
    c = pl.program_id(1)
    units = [(g, h) for g in range(group) for h in range(GLA_HEADS)]

    @pl.when(c == 0)
    def _():
        for g, h in units:
            if has_init:
                st_ref[g * GLA_HEADS + h] = s0_ref[g, h].T
            else:
                st_ref[g * GLA_HEADS + h] = jnp.zeros((GLA_DV, GLA_DK), F32)

    incl = _iota2((CHUNK, CHUNK), 0) >= _iota2((CHUNK, CHUNK), 1)
    tri = jnp.where(incl, 1.0, 0.0).astype(BF16)
    q_e, k_e, k_l, d_last = [], [], [], []
    for g in range(group):
        x = _dot(gl_ref[g].astype(BF16), wg2_ref[...]) + bg_ref[...]
        log_a = _log_sigmoid(x) * (1.0 / GLA_TAU)
        b = _tri_dot(tri, log_a)
        b_last = b[CHUNK - 1:CHUNK, :]
        q = q_ref[g] * (GLA_DK ** -0.5)
        k = k_ref[g]
        q_e.append((q * jnp.exp(b)).astype(BF16))
        k_e.append((k * jnp.exp(-b)).astype(BF16))
        k_l.append((k * jnp.exp(b_last - b)).astype(BF16))
        d_last.append(jnp.exp(b_last))
    gain = on_ref[...]
    ks = [slice(h * GLA_DK, (h + 1) * GLA_DK) for h in range(GLA_HEADS)]
    vs = [slice(h * GLA_DV, (h + 1) * GLA_DV) for h in range(GLA_HEADS)]
    v = [v_ref[g, :, vs[h]].astype(BF16) for g, h in units]
    st = [st_ref[g * GLA_HEADS + h] for g, h in units]
    att = [jnp.where(incl, _dot_nt(q_e[g][:, ks[h]], k_e[g][:, ks[h]]), 0.0).astype(BF16) for g, h in units]
    inter = [_dot_nt(q_e[g][:, ks[h]], st[u].astype(BF16)) for u, (g, h) in enumerate(units)]
    o = [_dot(att[u], v[u]) + inter[u] for u in range(len(units))]
    upd = [_dot_tn(v[u], k_l[g][:, ks[h]]) for u, (g, h) in enumerate(units)]
    for u, (g, h) in enumerate(units):
        st_ref[g * GLA_HEADS + h] = st[u] * d_last[g][:, ks[h]] + upd[u]
        o_ref[g, :, vs[h]] = (_rms(o[u], gain) * _silu(r_ref[g, :, vs[h]])).astype(BF16)

    @pl.when(c == pl.num_programs(1) - 1)
    def _():
        for g, h in units:
            sfin_ref[g, h] = st_ref[g * GLA_HEADS + h].T


def _gla(proj, w_gate2, b_gate, onorm, s0, nb, nc):
    qk = GLA_HEADS * GLA_DK
    vv = GLA_HEADS * GLA_DV
    proj = proj.reshape(nb, nc * CHUNK, proj.shape[1])
    blk = lambda width, col: pl.BlockSpec((GROUP, CHUNK, width), lambda g, c: (g, c, col))
    in_specs = [
        blk(qk, 0), blk(qk, 1), blk(vv, 1), blk(vv, 2), blk(LANES, (2 * qk + 2 * vv) // LANES),
        pl.BlockSpec((LANES, qk), lambda g, c: (0, 0)),
        pl.BlockSpec((1, qk), lambda g, c: (0, 0)),
        pl.BlockSpec((1, GLA_DV), lambda g, c: (0, 0)),
    ]
    args = [proj, proj, proj, proj, proj, w_gate2, b_gate, onorm]
    state = pl.BlockSpec((GROUP, GLA_HEADS, GLA_DK, GLA_DV), lambda g, c: (g, 0, 0, 0))
    if s0 is not None:
        in_specs.append(state)
        args.append(s0)
    o, s_fin = pl.pallas_call(
        functools.partial(_gla_kernel, has_init=s0 is not None, group=GROUP),
        out_shape=(jax.ShapeDtypeStruct((nb, nc * CHUNK, vv), BF16),
                   jax.ShapeDtypeStruct((nb, GLA_HEADS, GLA_DK, GLA_DV), F32)),
        grid=(nb // GROUP, nc),
        in_specs=in_specs,
        out_specs=(blk(vv, 0), state),
        scratch_shapes=[pltpu.VMEM((GROUP * GLA_HEADS, GLA_DV, GLA_DK), F32)],
        compiler_params=_params("parallel", "arbitrary"),
        name="gla",
    )(*args)
    return o.reshape(nb * nc * CHUNK, vv), s_fin


BIAS_WIDTH = 640


def _band_bias_kernel(w_ref, o_ref):
    for h in range(BAND_HEADS):
        x = jnp.broadcast_to(w_ref[h:h + 1, :], (CHUNK, BIAS_WIDTH))
        rolled = pltpu.roll(x, 0, 1, stride=1, stride_axis=0)
        o_ref[h // 2, (h % 2) * CHUNK:(h % 2 + 1) * CHUNK, :] = rolled[:, 0:BAND_SPAN]


def _band_bias(rel_bias):
    c = jnp.arange(BIAS_WIDTH)
    n = jnp.where(c <= BAND_SPAN, c, c - BIAS_WIDTH)
    w = rel_bias[:, jnp.clip(BAND_WINDOW - n, -REL_CLIP, REL_CLIP) + REL_CLIP]
    return pl.pallas_call(
        _band_bias_kernel,
        out_shape=jax.ShapeDtypeStruct((BAND_HEADS // 2, 2 * CHUNK, BAND_SPAN), F32),
        name="band_bias",
    )(w)


def _band_attn_kernel(q_ref, kp_ref, kc_ref, vp_ref, vc_ref, bias_ref, o_ref, kall_ref, vall_ref,
                      *, nq, prompt):
    if prompt:
        kall_ref[0:BAND_WINDOW] = kp_ref[...]
        vall_ref[0:BAND_WINDOW] = vp_ref[...]
    else:
        kall_ref[0:BAND_WINDOW] = kp_ref[0].astype(BF16)
        vall_ref[0:BAND_WINDOW] = vp_ref[0].astype(BF16)
    kall_ref[BAND_WINDOW:] = kc_ref[...]
    vall_ref[BAND_WINDOW:] = vc_ref[...]
    first_pos = (pl.program_id(1) - 1) * BAND_WINDOW if prompt else 0
    key_idx = _iota2((2 * CHUNK, BAND_SPAN), 1)
    pairs = range(D_MODEL // LANES)
    lanes = [slice(p * LANES, (p + 1) * LANES) for p in pairs]

    def chunk(t, carry):
        r0 = pl.multiple_of(t * CHUNK, CHUNK)
        qs = [_stack_pair(q_ref[pl.ds(r0, CHUNK), ls]) for ls in lanes]
        s = [_dot_nt(qs[p], kall_ref[pl.ds(r0, BAND_SPAN), lanes[p]]) + bias_ref[p]
             for p in pairs]
        if prompt:
            visible = first_pos + r0 + key_idx >= 0
            s = [jnp.where(visible, x, -jnp.inf) for x in s]
        e = [jnp.exp(x - jnp.max(x, axis=-1, keepdims=True)) for x in s]
        prob = [(x / jnp.sum(x, axis=-1, keepdims=True)).astype(BF16) for x in e]
        pv = [_dot(prob[p], vall_ref[pl.ds(r0, BAND_SPAN), lanes[p]]) for p in pairs]
        for p in pairs:
            o_ref[pl.ds(r0, CHUNK), lanes[p]] = _unstack_pair(pv[p]).astype(BF16)
        return carry

    lax.fori_loop(0, nq, chunk, 0)


def _band_attn_prompt(qn, knb, vb, bias, nb, seq):
    d = D_MODEL
    blk = BAND_WINDOW
    nblk = seq // blk
    cur = pl.BlockSpec((blk, d), lambda b, i: (b * nblk + i, 0))
    prev = pl.BlockSpec((blk, d), lambda b, i: (b * nblk + jnp.maximum(i - 1, 0), 0))
    return pl.pallas_call(
        functools.partial(_band_attn_kernel, nq=blk // CHUNK, prompt=True),
        out_shape=jax.ShapeDtypeStruct((nb * seq, d), BF16),
        grid=(nb, nblk),
        in_specs=[cur, prev, cur, prev, cur,
                  pl.BlockSpec(bias.shape, lambda b, i: (0, 0, 0))],
        out_specs=cur,
        scratch_shapes=[pltpu.VMEM((2 * blk, d), BF16), pltpu.VMEM((2 * blk, d), BF16)],
        compiler_params=_params("parallel", "arbitrary"),
        name="band_attn_prompt",
    )(qn, knb, knb, vb, vb, bias)


def _band_attn_sample(qn, knb, vb, cache_k, cache_v, bias, nb, row_off):
    d = D_MODEL
    new = pl.BlockSpec((CHUNK, d), lambda b: (row_off + b, 0))
    old = pl.BlockSpec((1, BAND_WINDOW, d), lambda b: (b, 0, 0))
    return pl.pallas_call(
        functools.partial(_band_attn_kernel, nq=1, prompt=False),
        out_shape=jax.ShapeDtypeStruct((nb * CHUNK, d), BF16),
        grid=(nb,),
        in_specs=[new, old, new, old, new, pl.BlockSpec(bias.shape, lambda b: (0, 0, 0))],
        out_specs=pl.BlockSpec((CHUNK, d), lambda b: (b, 0)),
        scratch_shapes=[pltpu.VMEM((BAND_SPAN, d), BF16), pltpu.VMEM((BAND_SPAN, d), BF16)],
        compiler_params=_params("parallel"),
        name="band_attn_sample",
    )(qn, cache_k, knb, cache_v, vb, bias)


def _gdn_prep_kernel(*refs, has_init):
    if has_init:
        x_ref, raw_ref, cw_ref, par_ref, c0_ref, q_ref, k_ref, v_ref, bg_ref, carry_ref = refs
    else:
        x_ref, raw_ref, cw_ref, par_ref, q_ref, k_ref, v_ref, bg_ref, carry_ref = refs
    tb = x_ref.shape[0]

    @pl.when(pl.program_id(1) == 0)
    def _():
        if has_init:
            carry_ref[...] = c0_ref[0]
        else:
            carry_ref[...] = jnp.zeros_like(carry_ref)

    row8 = _iota2((SUBLANES, LANES), 0)
    for j in range(GDN_CONV_CH // LANES):
        ls = slice(j * LANES, (j + 1) * LANES)
        w = cw_ref[:, ls]
        x8 = x_ref[0:SUBLANES, ls]
        c8 = carry_ref[:, ls]
        y8 = x8 * w[CONV_W - 1:CONV_W]
        for s in range(1, CONV_W):
            ws = w[CONV_W - 1 - s:CONV_W - s]
            y8 = y8 + jnp.where(row8 < s, pltpu.roll(c8, s, 0), pltpu.roll(x8, s, 0)) * ws
        carry_ref[:, ls] = x_ref[tb - SUBLANES:tb, ls]
        if tb > SUBLANES:
            y = x_ref[SUBLANES:tb, ls] * w[CONV_W - 1:CONV_W]
            for s in range(1, CONV_W):
                y = y + x_ref[SUBLANES - s:tb - s, ls] * w[CONV_W - 1 - s:CONV_W - s]
            y = jnp.concatenate([y8, y], axis=0)
        else:
            y = y8
        y = _silu(y)
        h = j % GDN_HEADS
        if j < 2 * GDN_HEADS:
            y = y * lax.rsqrt(jnp.sum(y * y, axis=-1, keepdims=True) + EPS)
        (q_ref, k_ref, v_ref)[j // GDN_HEADS][h] = y
    raw = raw_ref[...]
    beta = _sigmoid(raw)
    g = par_ref[0:1, :] * _softplus(raw + par_ref[1:2, :])
    bg_ref[...] = jnp.where(_iota2(raw.shape, 1) < GDN_HEADS, beta, g)


def _gdn_prep(proj, conv_w, par, c0, nb, seq, row_off, tb):
    nblk = seq // tb
    base = row_off // tb
    row = lambda b, i: base + b * nblk + i
    in_specs = [
        pl.BlockSpec((tb, GDN_CONV_CH), lambda b, i: (row(b, i), 0)),
        pl.BlockSpec((tb, LANES), lambda b, i: (row(b, i), (GDN_CONV_CH + GDN_HEADS * GDN_DV) // LANES)),
        pl.BlockSpec((CONV_W, GDN_CONV_CH), lambda b, i: (0, 0)),
        pl.BlockSpec((2, LANES), lambda b, i: (0, 0)),
    ]
    args = [proj, proj, conv_w, par]
    if c0 is not None:
        in_specs.append(pl.BlockSpec((1, SUBLANES, GDN_CONV_CH), lambda b, i: (b, 0, 0)))
        args.append(c0)
    n = nb * seq
    head = pl.BlockSpec((GDN_HEADS, tb, GDN_DK), lambda b, i: (0, b * nblk + i, 0))
    return pl.pallas_call(
        functools.partial(_gdn_prep_kernel, has_init=c0 is not None),
        out_shape=(jax.ShapeDtypeStruct((GDN_HEADS, n, GDN_DK), F32),
                   jax.ShapeDtypeStruct((GDN_HEADS, n, GDN_DK), F32),
                   jax.ShapeDtypeStruct((GDN_HEADS, n, GDN_DV), F32),
                   jax.ShapeDtypeStruct((n, LANES), F32)),
        grid=(nb, nblk),
        in_specs=in_specs,
        out_specs=(head, head, head, pl.BlockSpec((tb, LANES), lambda b, i: (b * nblk + i, 0))),
        scratch_shapes=[pltpu.VMEM((SUBLANES, GDN_CONV_CH), F32)],
        compiler_params=_params("parallel", "arbitrary"),
        name="gdn_prep",
    )(*args)


def _gdn_kernel(*refs, has_init, group):
    if has_init:
        q_ref, k_ref, v_ref, bg_ref, gate_ref, on_ref, s0_ref, o_ref, sfin_ref, s_ref = refs
    else:
        q_ref, k_ref, v_ref, bg_ref, gate_ref, on_ref, o_ref, sfin_ref, s_ref = refs
    c = pl.program_id(1)

    @pl.when(c == 0)
    def _():
        if has_init:
            s_ref[...] = s0_ref[...].reshape(s_ref.shape)
        else:
            s_ref[...] = jnp.zeros_like(s_ref)

    ri = _iota2((CHUNK, CHUNK), 0)
    ci = _iota2((CHUNK, CHUNK), 1)
    incl = ri >= ci
    strict = ri > ci
    tri = jnp.where(incl, 1.0, 0.0).astype(BF16)
    eye = jnp.where(ri == ci, 1.0, 0.0)
    gain = on_ref[...]
    units = [(g, h) for g in range(group) for h in range(GDN_HEADS)]
    heads = range(len(units))
    bg = [bg_ref[g] for g in range(group)]
    cs = [_tri_dot(tri, x) for x in bg]
    cs_t = [x.T for x in cs]
    beta = [bg[g][:, h:h + 1] for g, h in units]
    gcol = [cs[g][:, GDN_HEADS + h:GDN_HEADS + h + 1] for g, h in units]
    glast = [x[CHUNK - 1:CHUNK, :] for x in gcol]
    gam = [jnp.exp(jnp.where(incl, gcol[u] - cs_t[g][GDN_HEADS + h:GDN_HEADS + h + 1, :], -jnp.inf))
           for u, (g, h) in enumerate(units)]
    egc = [jnp.exp(x) for x in gcol]
    k = [k_ref[h, g] for g, h in units]
    k16 = [x.astype(BF16) for x in k]
    kb = [k[h] * beta[h] for h in heads]
    m = [jnp.where(strict, _dot_nt(kb[h].astype(BF16), k16[h]) * gam[h], 0.0) for h in heads]
    d = [eye - jnp.where((ri == ci + 1) & ((ci & 1) == 0), m[h], 0.0) for h in heads]
    for lvl in range(1, 6):
        sub = (((ri >> lvl) & 1) == 1) & (((ci >> lvl) & 1) == 0) & ((ri >> (lvl + 1)) == (ci >> (lvl + 1)))
        y = [_dot_f32(d[h], jnp.where(sub, m[h], 0.0)) for h in heads]
        x = [_dot_f32(y[h], d[h]) for h in heads]
        d = [d[h] - x[h] for h in heads]
    t16 = [t.astype(BF16) for t in d]
    u = [_dot(t16[i], (v_ref[h, g] * beta[i]).astype(BF16)) for i, (g, h) in enumerate(units)]
    w = [_dot(t16[h], (kb[h] * egc[h]).astype(BF16)) for h in heads]
    q = [q_ref[h, g] * (GDN_DK ** -0.5) for g, h in units]
    att = [_dot_nt(q[h].astype(BF16), k16[h]) * gam[h] for h in heads]
    q_e = [(q[h] * egc[h]).astype(BF16) for h in heads]
    k_l = [(k[h] * jnp.exp(glast[h] - gcol[h])).astype(BF16) for h in heads]
    s = [s_ref[h] for h in heads]
    s16 = [x.astype(BF16) for x in s]
    v_new = [u[h] - _dot(w[h].astype(BF16), s16[h]) for h in heads]
    v16 = [x.astype(BF16) for x in v_new]
    o = [_dot(q_e[h], s16[h]) + _dot(att[h].astype(BF16), v16[h]) for h in heads]
    for i, (g, h) in enumerate(units):
        s_ref[i] = s[i] * jnp.exp(glast[i]) + _dot_tn(k_l[i], v16[i])
        vs = slice(h * GDN_DV, (h + 1) * GDN_DV)
        o_ref[g, :, vs] = (_rms(o[i], gain) * _silu(gate_ref[g, :, vs])).astype(BF16)

    @pl.when(c == pl.num_programs(1) - 1)
    def _():
        sfin_ref[...] = s_ref[...].reshape(sfin_ref.shape)


def _gdn(qh, kh, vh, bg, proj, onorm, s0, nb, nc):
    vv = GDN_HEADS * GDN_DV
    seq = nc * CHUNK
    qh, kh, vh = (t.reshape(GDN_HEADS, nb, seq, t.shape[-1]) for t in (qh, kh, vh))
    bg = bg.reshape(nb, seq, LANES)
    proj = proj.reshape(nb, seq, proj.shape[1])
    head = pl.BlockSpec((GDN_HEADS, GROUP, CHUNK, GDN_DK), lambda g, c: (0, g, c, 0))
    in_specs = [
        head, head, head,
        pl.BlockSpec((GROUP, CHUNK, LANES), lambda g, c: (g, c, 0)),
        pl.BlockSpec((GROUP, CHUNK, vv), lambda g, c: (g, c, GDN_CONV_CH // vv)),
        pl.BlockSpec((1, GDN_DV), lambda g, c: (0, 0)),
    ]
    args = [qh, kh, vh, bg, proj, onorm]
    state = pl.BlockSpec((GROUP, GDN_HEADS, GDN_DK, GDN_DV), lambda g, c: (g, 0, 0, 0))
    if s0 is not None:
        in_specs.append(state)
        args.append(s0)
    o, s_fin = pl.pallas_call(
        functools.partial(_gdn_kernel, has_init=s0 is not None, group=GROUP),
        out_shape=(jax.ShapeDtypeStruct((nb, seq, vv), BF16),
                   jax.ShapeDtypeStruct((nb, GDN_HEADS, GDN_DK, GDN_DV), F32)),
        grid=(nb // GROUP, nc),
        in_specs=in_specs,
        out_specs=(pl.BlockSpec((GROUP, CHUNK, vv), lambda g, c: (g, c, 0)), state),
        scratch_shapes=[pltpu.VMEM((GROUP * GDN_HEADS, GDN_DK, GDN_DV), F32)],
        compiler_params=_params("parallel", "arbitrary"),
        name="gdn",
    )(*args)
    return o.reshape(nb * seq, vv), s_fin


def _neg_abs(x):
    bits = lax.bitcast_convert_type(x, jnp.uint32) | jnp.uint32(0x80000000)
    return lax.bitcast_convert_type(bits, F32)


def _stack_pair(q2):
    low = _iota2(q2.shape, 1) < SB_DH
    return jnp.concatenate([jnp.where(low, q2, 0), jnp.where(low, 0, q2)], axis=0)


def _unstack_pair(acc):
    tq = acc.shape[0] // 2
    low = _iota2((tq, LANES), 1) < SB_DH
    return jnp.where(low, acc[:tq], acc[tq:])


def _upper(tk):
    return jnp.where(_iota2((tk, tk), 0) > _iota2((tk, tk), 1), 1.0, 0.0).astype(BF16)


def _diag_mask(tq, tk):
    r = _iota2((2 * tq, tk), 0)
    r = jnp.where(r >= tq, r - tq, r)
    return _iota2((2 * tq, tk), 1) < r


SB_NEUTRAL = 1e30
SB_LAG = 4
SB_CHAINS = 1


def _sb_log_terms(zn, mask):
    l1m = jnp.minimum(zn, 0.0) - jnp.log2(1.0 + jnp.exp2(_neg_abs(zn)))
    log_b = l1m - zn
    if mask is not None:
        l1m = jnp.where(mask, l1m, 0.0)
        log_b = jnp.where(mask, log_b, -SB_NEUTRAL)
    return l1m, log_b


def _sb_direct(qs, k2, v2, mask):
    l1m, log_b = _sb_log_terms(_dot_nt(qs, k2), mask)
    a = jnp.exp2(log_b + _dot(l1m.astype(BF16), _upper(k2.shape[0])))
    return jnp.sum(l1m, axis=-1, keepdims=True), _dot(a.astype(BF16), v2)


def _sb_pipeline(nhalf, nchain, q_of, k_of, v_of, st_of, upper, scratch, mask):
    z_ref, hi_ref, logb_ref, tot_ref, cs_ref, a_ref, run_ref, acc_ref = scratch
    logb_ref[...] = jnp.full(logb_ref.shape, -SB_NEUTRAL, F32)
    tot_ref[...] = jnp.zeros(tot_ref.shape, F32)
    for c in range(nchain):
        z_ref[2 * c + 1] = jnp.full(z_ref.shape[1:], SB_NEUTRAL, F32)
        hi_ref[2 * c + 1] = jnp.zeros(hi_ref.shape[1:], BF16)
        cs_ref[2 * c + 1] = jnp.zeros(cs_ref.shape[1:], F32)
        a_ref[2 * c + 1] = jnp.zeros(a_ref.shape[1:], BF16)

    def half(h, p):
        for c in range(nchain):
            rows = st_of(h - 4, c)
            acc_ref[rows, :] += _dot(a_ref[2 * c + 1 - p], v_of(h - 4, c))
        for c in range(nchain):
            rows = st_of(h - 3, c)
            run = run_ref[rows, :]
            a_ref[2 * c + p] = jnp.exp2(cs_ref[2 * c + 1 - p] + (logb_ref[2 * c + p] + run)).astype(BF16)
            run_ref[rows, :] = run + tot_ref[2 * c + p]
        for c in range(nchain):
            cs_ref[2 * c + p] = _dot(hi_ref[2 * c + 1 - p], upper)
        for c in range(nchain):
            l1m, logb_ref[2 * c + p] = _sb_log_terms(z_ref[2 * c + 1 - p], mask)
            hi_ref[2 * c + p] = l1m.astype(BF16)
            tot_ref[2 * c + p] = jnp.sum(l1m, axis=-1, keepdims=True)
        for c in range(nchain):
            z_ref[2 * c + p] = _dot_nt(q_of(h, c), k_of(h, c))

    def body(i, _):
        half(2 * i, 0)
        half(2 * i + 1, 1)
        return 0

    assert nhalf % 2 == 0
    lax.fori_loop(0, nhalf // 2, body, 0)


def _sb_rings(nchain, rows, tk):
    slots = 2 * nchain
    return [pltpu.VMEM((slots, rows, tk), F32), pltpu.VMEM((slots, rows, tk), BF16),
            pltpu.VMEM((slots, rows, tk), F32), pltpu.VMEM((slots, rows, 1), F32),
            pltpu.VMEM((slots, rows, tk), F32), pltpu.VMEM((slots, rows, tk), BF16)]


def _sb_prompt_items(nq):
    diag = [(i, i) for i in range(nq)]
    off = [(j, i) for j in range(nq - 2, -1, -1) for i in range(j + 1, nq)]
    pad = [(0, nq)] * (SB_CHAINS * SB_LAG)
    assert len(diag) % (2 * SB_CHAINS) == 0 and len(off) % (2 * SB_CHAINS) == 0
    rows = []
    for items in (diag, off):
        padded = pad + items + pad
        rows += [[j for j, _ in padded], [i for _, i in padded]]
    width = max(len(r) for r in rows)
    tab = np.zeros((4, width), np.int32)
    for r, row in enumerate(rows):
        tab[r, :len(row)] = row
    return jnp.asarray(tab), len(diag), len(off)


def _sb_prompt_kernel(tab_ref, q_ref, k_ref, v_ref, o_ref, qs_ref, run_ref, acc_ref, *rings, nq, ndiag, noff):
    tk = SB_TK
    rows = 2 * tk

    def stack(i, _):
        qs_ref[pl.ds(pl.multiple_of(i * rows, rows), rows), :] = _stack_pair(
            q_ref[pl.ds(pl.multiple_of(i * tk, tk), tk), :])
        return 0

    lax.fori_loop(0, nq, stack, 0)
    qs_ref[nq * rows:(nq + 1) * rows, :] = jnp.zeros((rows, LANES), BF16)
    run_ref[...] = jnp.zeros(run_ref.shape, F32)
    acc_ref[...] = jnp.zeros(acc_ref.shape, F32)
    upper = _upper(tk)
    scratch = list(rings) + [run_ref, acc_ref]
    for row, nitems, mask in ((0, ndiag, _diag_mask(tk, tk)), (2, noff, None)):
        col = lambda hh, c: SB_CHAINS * (hh + SB_LAG) + c
        q_rows = lambda hh, c: pl.ds(pl.multiple_of(tab_ref[row + 1, col(hh, c)] * rows, rows), rows)
        k_rows = lambda hh, c: pl.ds(pl.multiple_of(tab_ref[row, col(hh, c)] * tk, tk), tk)
        _sb_pipeline(nitems // SB_CHAINS + SB_LAG, SB_CHAINS,
                     lambda hh, c: qs_ref[q_rows(hh, c), :],
                     lambda hh, c: k_ref[k_rows(hh, c), :],
                     lambda hh, c: v_ref[k_rows(hh, c), :],
                     q_rows, upper, scratch, mask)

    def emit(i, _):
        o_ref[pl.ds(pl.multiple_of(i * tk, tk), tk), :] = _unstack_pair(
            acc_ref[pl.ds(pl.multiple_of(i * rows, rows), rows), :]).astype(BF16)
        return 0

    lax.fori_loop(0, nq, emit, 0)


def _sb_prompt(qb, kb, vb, nb, seq):
    d = D_MODEL
    npair = d // LANES
    nq = seq // SB_TK
    rows = 2 * SB_TK
    tab, ndiag, noff = _sb_prompt_items(nq)
    col = pl.BlockSpec((seq, LANES), lambda b, p, tab: (b, p))
    return pl.pallas_call(
        functools.partial(_sb_prompt_kernel, nq=nq, ndiag=ndiag, noff=noff),
        out_shape=jax.ShapeDtypeStruct((nb * seq, d), BF16),
        grid_spec=pltpu.PrefetchScalarGridSpec(
            num_scalar_prefetch=1,
            grid=(nb, npair),
            in_specs=[col, col, col],
            out_specs=col,
            scratch_shapes=[pltpu.VMEM(((nq + 1) * rows, LANES), BF16), pltpu.VMEM(((nq + 1) * rows, 1), F32),
                            pltpu.VMEM(((nq + 1) * rows, LANES), F32)] + _sb_rings(SB_CHAINS, rows, SB_TK)),
        compiler_params=_params("parallel", "arbitrary"),
        name="sb_prompt",
    )(tab, qb, kb, vb)


def _sb_sample_kernel(q_ref, kn_ref, vn_ref, kc_ref, vc_ref, o_ref, run_ref, acc_ref, *rings):
    tq, tk = CHUNK, SB_TK
    rows = 2 * tq
    nblk = kc_ref.shape[1] // tk
    lanes = lambda c: slice(c * LANES, (c + 1) * LANES)
    qs = [_stack_pair(q_ref[:, lanes(c)]) for c in range(2)]
    for c in range(2):
        run, acc = _sb_direct(qs[c], kn_ref[:, lanes(c)], vn_ref[:, lanes(c)], _diag_mask(tq, tq))
        run_ref[c * rows:(c + 1) * rows, :] = run
        acc_ref[c * rows:(c + 1) * rows, :] = acc

    def k_rows(hh):
        return pl.ds(pl.multiple_of(jnp.clip(nblk - 1 - hh, 0, nblk - 1) * tk, tk), tk)

    def st_of(hh, c):
        real = (hh >= 0) & (hh < nblk)
        return pl.ds(pl.multiple_of(jnp.where(real, c, 2) * rows, rows), rows)

    _sb_pipeline(nblk + SB_LAG, 2,
                 lambda hh, c: qs[c],
                 lambda hh, c: kc_ref[0, k_rows(hh), lanes(c)].astype(BF16),
                 lambda hh, c: vc_ref[0, k_rows(hh), lanes(c)].astype(BF16),
                 st_of, _upper(tk), list(rings) + [run_ref, acc_ref], None)
    for c in range(2):
        o_ref[:, lanes(c)] = _unstack_pair(acc_ref[c * rows:(c + 1) * rows, :]).astype(BF16)


def _sb_sample(qb, kb, vb, cache_k, cache_v, nb, row_off):
    d = D_MODEL
    wide = 2 * LANES
    ngrp = d // wide
    past = cache_k.shape[1]
    rows = 2 * CHUNK
    new = pl.BlockSpec((CHUNK, wide), lambda b, g: (row_off + b, g))
    old = pl.BlockSpec((1, past, wide), lambda b, g: (b, 0, g))
    return pl.pallas_call(
        _sb_sample_kernel,
        out_shape=jax.ShapeDtypeStruct((nb * CHUNK, d), BF16),
        grid=(nb, ngrp),
        in_specs=[new, new, new, old, old],
        out_specs=pl.BlockSpec((CHUNK, wide), lambda b, g: (b, g)),
        scratch_shapes=[pltpu.VMEM((3 * rows, 1), F32), pltpu.VMEM((3 * rows, LANES), F32)]
        + _sb_rings(2, rows, SB_TK),
        compiler_params=_params("parallel", "parallel"),
        name="sb_sample",
    )(qb, kb, vb, cache_k, cache_v)


def _pad_cols(w, n):
    return jnp.pad(w, ((0, 0), (0, n - w.shape[1])))


def kernel(x_prompt, x_sample, state_gla, cache_band_k, cache_band_v, state_gdn, state_gdn_conv,
           cache_sb_k, cache_sb_v, ffn1_norm, ffn1_w_gu, ffn1_w_down, mix_norm, ffn2_norm,
           ffn2_w_gu, ffn2_w_down, gla_w_in, gla_w_gate2, gla_b_gate, gla_onorm, gla_w_out,
           band_w_in, band_q_norm, band_k_norm, band_rel_bias, band_w_out, gdn_w_in, gdn_conv_w,
           gdn_a_log, gdn_dt_bias, gdn_onorm, gdn_w_out, sb_w_in, sb_w_out):
    bp, seq, d = x_prompt.shape
    bs, dec = x_sample.shape[:2]
    assert dec == CHUNK and seq % BAND_WINDOW == 0 and d == D_MODEL
    n_p = bp * seq
    n_s = bs * dec
    nc_p = seq // CHUNK
    x = (x_prompt.reshape(n_p, d), x_sample.reshape(n_s, d))
    depth = ffn1_norm.shape[0]
    outs = {}
    for i in range(depth):
        m, j = i % 4, i // 4
        x = _ffn(x, ffn1_norm[i], ffn1_w_gu[i].astype(BF16), ffn1_w_down[i].astype(BF16))
        if m == 0:
            n_in = 2 * GLA_HEADS * GLA_DK + 2 * GLA_HEADS * GLA_DV + LANES
            gla_proj = functools.partial(_proj, x, mix_norm[i], _pad_cols(gla_w_in[j], n_in).astype(BF16),
                                         [(n_in, F32)], _col_segs(n_in, lambda c: [(0, c, 1.0)]))
            wg2 = jnp.pad(gla_w_gate2[j], ((0, LANES - GLA_RANK), (0, 0))).astype(BF16)
            common = (wg2, gla_b_gate[j].reshape(1, -1), gla_onorm[j].reshape(1, -1))
            o_p, s_p = _gla(*gla_proj(rows=(0, n_p)), *common, None, bp, nc_p)
            o_s, s_s = _gla(*gla_proj(rows=(n_p, n_p + n_s)), *common, state_gla[j], bs, 1)
            outs.setdefault("gla", []).append((s_p, s_s))
            w_out = gla_w_out[j]
        elif m == 1:
            band_proj = functools.partial(
                _proj, x, mix_norm[i], band_w_in[j].astype(BF16),
                [(d, BF16), (d, F32), (d, BF16), (d, F32), (d, BF16)],
                [(0, d, 0, [(0, 0, BAND_DH ** -0.5)]), (d, 2 * d, 1, [(1, 0, 1.0), (2, 0, 1.0)]),
                 (2 * d, 3 * d, None, [(3, 0, 1.0), (4, 0, 1.0)])],
                gains=(band_q_norm[j], band_k_norm[j]))
            qn_p, kn_p, knb_p, v_p, vb_p = band_proj(rows=(0, n_p))
            qn_s, kn_s, knb_s, v_s, vb_s = band_proj(rows=(n_p, n_p + n_s))
            bias = _band_bias(band_rel_bias[j])
            o_p = _band_attn_prompt(qn_p, knb_p, vb_p, bias, bp, seq)
            o_s = _band_attn_sample(qn_s, knb_s, vb_s, cache_band_k[j].reshape(bs, BAND_WINDOW, d),
                                    cache_band_v[j].reshape(bs, BAND_WINDOW, d), bias, bs, 0)
            keep = min(BAND_WINDOW, seq)
            shp = (BAND_HEADS, BAND_DH)
            tail = lambda a: jnp.stack([a[(b + 1) * seq - keep:(b + 1) * seq] for b in range(bp)]).reshape(bp, keep, *shp)
            outs.setdefault("band", []).append((
                tail(kn_p), tail(v_p), kn_s.reshape(bs, dec, *shp), v_s.reshape(bs, dec, *shp)))
            w_out = band_w_out[j]
        elif m == 2:
            n_in = GDN_CONV_CH + GDN_HEADS * GDN_DV + LANES
            gdn_proj = functools.partial(_proj, x, mix_norm[i], _pad_cols(gdn_w_in[j], n_in).astype(BF16),
                                         [(n_in, F32)], _col_segs(n_in, lambda c: [(0, c, 1.0)]))
            (proj_p,), (proj_s,) = gdn_proj(rows=(0, n_p)), gdn_proj(rows=(n_p, n_p + n_s))
            par = jnp.zeros((2, LANES), F32)
            par = par.at[0, GDN_HEADS:2 * GDN_HEADS].set(-jnp.exp(gdn_a_log[j]))
            par = par.at[1, GDN_HEADS:2 * GDN_HEADS].set(gdn_dt_bias[j])
            c0 = jnp.pad(state_gdn_conv[j], ((0, 0), (SUBLANES - (CONV_W - 1), 0), (0, 0)))
            pre_p = _gdn_prep(proj_p, gdn_conv_w[j], par, None, bp, seq, 0, ROW_TILE)
            pre_s = _gdn_prep(proj_s, gdn_conv_w[j], par, c0, bs, dec, 0, dec)
            onorm = gdn_onorm[j].reshape(1, -1)
            o_p, s_p = _gdn(*pre_p, proj_p, onorm, None, bp, nc_p)
            o_s, s_s = _gdn(*pre_s, proj_s, onorm, state_gdn[j], bs, 1)
            last = CONV_W - 1
            conv_p = jnp.stack([proj_p[(b + 1) * seq - last:(b + 1) * seq, :GDN_CONV_CH] for b in range(bp)])
            conv_s = proj_s[:, :GDN_CONV_CH].reshape(bs, dec, -1)[:, dec - last:]
            outs.setdefault("gdn", []).append((s_p, conv_p, s_s, conv_s))
            w_out = gdn_w_out[j]
        else:
            sb_proj = functools.partial(
                _proj, x, mix_norm[i], sb_w_in[j].astype(BF16),
                [(d, BF16), (d, F32), (d, BF16), (d, F32), (d, BF16)],
                [(0, d, None, [(0, 0, -LOG2E * SB_DH ** -0.5)]), (d, 2 * d, None, [(1, 0, 1.0), (2, 0, 1.0)]),
                 (2 * d, 3 * d, None, [(3, 0, 1.0), (4, 0, 1.0)])])
            qb_p, k_p, kb_p, v_p, vb_p = sb_proj(rows=(0, n_p))
            qb_s, k_s, kb_s, v_s, vb_s = sb_proj(rows=(n_p, n_p + n_s))
            o_p = _sb_prompt(qb_p, kb_p, vb_p, bp, seq)
            o_s = _sb_sample(qb_s, kb_s, vb_s, cache_sb_k[j].reshape(bs, -1, d),
                             cache_sb_v[j].reshape(bs, -1, d), bs, 0)
            shp = (SB_HEADS, SB_DH)
            outs.setdefault("sb", []).append((
                k_p.reshape(bp, seq, *shp), v_p.reshape(bp, seq, *shp),
                k_s.reshape(bs, dec, *shp), v_s.reshape(bs, dec, *shp)))
            w_out = sb_w_out[j]
        x = _ffn(x, ffn2_norm[i], ffn2_w_gu[i].astype(BF16), ffn2_w_down[i].astype(BF16),
                 mix=(o_p, o_s, w_out.astype(BF16)))
    stack = lambda key, idx: jnp.stack([t[idx] for t in outs[key]])
    return (x[:n_p].reshape(bp, seq, d), x[n_p:].reshape(bs, dec, d),
            stack("gla", 0), stack("gla", 1),
            stack("band", 0), stack("band", 1), stack("band", 2), stack("band", 3),
            stack("gdn", 0), stack("gdn", 1), stack("gdn", 2), stack("gdn", 3),
            stack("sb", 0), stack("sb", 1), stack("sb", 2), stack("sb", 3))
```
